```python
import jax, jax.numpy as jnp
from jax import lax
import numpy as np

D_MODEL = 1024
BATCH = 1
SEQ = 16384
DEPTH = 1
DEC_BATCH = 128
DEC_SEQ = 4
PAST_LEN = 8192
PAGE_SIZE = 128

N_HEADS = 8
HEAD_DIM = 64
ATTN_WIDTH = N_HEADS * HEAD_DIM
POOL_WINDOWS = (2, 4, 8, 16)
POOL_GROUPS = len(POOL_WINDOWS)
POOL_WIDTH = D_MODEL // 2
POOL_GROUP_WIDTH = POOL_WIDTH // POOL_GROUPS
POOL_HIST = max(POOL_WINDOWS) - 1
N_META = 16
BLOCK_Q = 128
N_EXPERT_GROUPS = 4
EXPERTS_PER_GROUP = 4
N_EXPERTS = N_EXPERT_GROUPS * EXPERTS_PER_GROUP
TOP_K_IN_GROUP = 2
D_EXPERT = D_MODEL // 2
D_IN_PROJ = 3 * ATTN_WIDTH + N_HEADS + POOL_WIDTH + 2 * D_MODEL
SPLIT_POINTS = [ATTN_WIDTH, 2 * ATTN_WIDTH, 3 * ATTN_WIDTH, 3 * ATTN_WIDTH + N_HEADS,
                3 * ATTN_WIDTH + N_HEADS + POOL_WIDTH, 3 * ATTN_WIDTH + N_HEADS + POOL_WIDTH + D_MODEL]
FORGET_BIAS_MIN = 1.0
FORGET_BIAS_MAX = 9.0
RMS_EPS = 1e-6

kernel_name = "fox_pool_hmoe_hybrid_step"


def rmsnorm(x, g):
    xf = x.astype(jnp.float32)
    inv = lax.rsqrt(jnp.mean(xf * xf, axis=-1, keepdims=True) + RMS_EPS)
    return (xf * inv).astype(x.dtype) * g


def attend(q, k, v, bias):
    s = jnp.einsum('bqhd,bkhd->bhqk', q, k).astype(jnp.float32) * (HEAD_DIM ** -0.5) + bias
    p = jax.nn.softmax(s, axis=-1).astype(v.dtype)
    return jnp.einsum('bhqk,bkhd->bqhd', p, v)


def mixer_inputs(h, norm_g, w_in_l, b_f_l):
    xn = rmsnorm(h, norm_g)
    proj = xn @ w_in_l
    q, k, v, f, u, ga, gb = jnp.split(proj, SPLIT_POINTS, axis=-1)
    hs = h.shape[:-1] + (N_HEADS, HEAD_DIM)
    logf = jax.nn.log_sigmoid((f + b_f_l).astype(jnp.float32))
    return q.reshape(hs), k.reshape(hs), v.reshape(hs), logf, u, ga, gb


def fox_prompt(q, k, v, logf):
    B, L = q.shape[0], q.shape[1]
    c = lax.cumsum(logf, axis=1)
    ck = jnp.transpose(c, (0, 2, 1))[:, :, None, :]
    pos_k = jnp.arange(L)

    def block(qb, cqb, pqb):
        cq = jnp.transpose(cqb, (0, 2, 1))[..., None]
        bias = jnp.where(pqb[:, None] >= pos_k[None, :], cq - ck, -jnp.inf)
        return attend(qb, k, v, bias)

    out_meta = block(q[:, :N_META], c[:, :N_META], pos_k[:N_META])
    nb = (L - N_META) // BLOCK_Q
    qr = jnp.moveaxis(q[:, N_META:].reshape(B, nb, BLOCK_Q, N_HEADS, HEAD_DIM), 1, 0)
    cr = jnp.moveaxis(c[:, N_META:].reshape(B, nb, BLOCK_Q, N_HEADS), 1, 0)
    pr = pos_k[N_META:].reshape(nb, BLOCK_Q)
    out_r = lax.map(lambda a: block(a[0], a[1], a[2]), (qr, cr, pr))
    out_r = jnp.moveaxis(out_r, 0, 1).reshape(B, L - N_META, N_HEADS, HEAD_DIM)
    return jnp.concatenate([out_meta, out_r], axis=1).reshape(B, L, ATTN_WIDTH)


def fox_sample(q, k, v, logf, cache_k, cache_v, cache_logf, layer, page_table):
    DB, T = q.shape[0], q.shape[1]
    tri = jnp.arange(T)[:, None] >= jnp.arange(T)[None, :]

    def one(a):
        qb, kb, vb, lfb, prow = a
        n_past = prow.shape[0] * PAGE_SIZE
        kp = cache_k[layer, prow].reshape(n_past, N_HEADS, HEAD_DIM)
        vp = cache_v[layer, prow].reshape(n_past, N_HEADS, HEAD_DIM)
        lfp = cache_logf[layer, prow].reshape(n_past, N_HEADS).astype(jnp.float32)
        suffix = lax.cumsum(lfp, axis=0, reverse=True) - lfp
        cnew = lax.cumsum(lfb, axis=0)
        bias_past = cnew[:, None, :] + suffix[None, :, :]
        bias_new = jnp.where(tri[:, :, None], cnew[:, None, :] - cnew[None, :, :], -jnp.inf)
        bias = jnp.transpose(jnp.concatenate([bias_past, bias_new], axis=1), (2, 0, 1))[None]
        keys = jnp.concatenate([kp, kb.astype(kp.dtype)], axis=0)[None]
        vals = jnp.concatenate([vp, vb.astype(vp.dtype)], axis=0)[None]
        return attend(qb[None], keys, vals, bias)[0]

    out = lax.map(one, (q, k, v, logf, page_table))
    return out.reshape(DB, T, ATTN_WIDTH)


def multiscale_pool(u, hist, pos0, pool_w_l, pool_scale_l):
    B, T = u.shape[0], u.shape[1]
    P = hist.shape[1]
    ext = jnp.concatenate([hist.astype(u.dtype), u], axis=1).astype(jnp.float32)
    cs = jnp.pad(lax.cumsum(ext, axis=1), ((0, 0), (1, 0), (0, 0)))
    upper = cs[:, P + 1:]
    pos = pos0 + jnp.arange(T)
    outs = []
    for g, w in enumerate(POOL_WINDOWS):
        sl = slice(g * POOL_GROUP_WIDTH, (g + 1) * POOL_GROUP_WIDTH)
        lower = cs[:, P + 1 - w:P + 1 - w + T, sl]
        cnt = jnp.minimum(pos + 1, w).astype(jnp.float32)
        mean = (upper[:, :, sl] - lower) / cnt[None, :, None]
        outs.append(mean - ext[:, P:, sl])
    d = jnp.stack(outs, axis=2).astype(u.dtype)
    mixed = jnp.einsum('btgc,gcd->btgd', d, pool_w_l).reshape(B, T, POOL_WIDTH)
    return mixed * pool_scale_l


def merge_branches(h, a, p, ga, gb, w_ba, w_bp, w_out_l):
    m = jax.nn.sigmoid(ga) * (a @ w_ba) + jax.nn.sigmoid(gb) * (p @ w_bp)
    return h + m @ w_out_l


def hier_moe(x, w_rg, b_rg, w_re, b_re, w_gate_l, w_up_l, w_down_l):
    N = x.shape[0]
    pg = jax.nn.softmax((x @ w_rg + b_rg).astype(jnp.float32), axis=-1)
    gsel = jnp.argmax(pg, axis=-1)
    pg_sel = jnp.take_along_axis(pg, gsel[:, None], axis=1)
    el = (x @ w_re + b_re).astype(jnp.float32).reshape(N, N_EXPERT_GROUPS, EXPERTS_PER_GROUP)
    el_sel = jnp.take_along_axis(el, gsel[:, None, None], axis=1)[:, 0]
    vals, idx = lax.top_k(el_sel, TOP_K_IN_GROUP)
    w2 = jax.nn.softmax(vals, axis=-1) * pg_sel
    eidx = gsel[:, None] * EXPERTS_PER_GROUP + idx
    combine = jnp.sum(jax.nn.one_hot(eidx, N_EXPERTS, dtype=x.dtype) * w2[..., None].astype(x.dtype), axis=1)
    y = jnp.zeros_like(x)
    for e in range(N_EXPERTS):
        hdn = jax.nn.silu(x @ w_gate_l[e]) * (x @ w_up_l[e])
        y = y + combine[:, e:e + 1] * (hdn @ w_down_l[e])
    return y


def ffn_block(h, g2, w_rg, b_rg, w_re, b_re, w_gate_l, w_up_l, w_down_l):
    xn = rmsnorm(h, g2).reshape(-1, D_MODEL)
    return h + hier_moe(xn, w_rg, b_rg, w_re, b_re, w_gate_l, w_up_l, w_down_l).reshape(h.shape)


def setup_inputs(seed: int = 0) -> dict:
    key = jax.random.key(seed)
    ks = jax.random.split(key, 32)
    n_pages = PAST_LEN // PAGE_SIZE
    n_pool = (DEC_BATCH * n_pages * 5) // 4

    def nrm(k, shape, scale=1.0):
        return jax.random.normal(k, shape, jnp.float32) * scale

    head_bias = jnp.linspace(FORGET_BIAS_MIN, FORGET_BIAS_MAX, N_HEADS, dtype=jnp.float32)
    page_table = jax.random.permutation(ks[6], n_pool)[:DEC_BATCH * n_pages].reshape(DEC_BATCH, n_pages).astype(jnp.int32)
    return {
        "x_prompt": nrm(ks[0], (BATCH, SEQ, D_MODEL)),
        "x_sample": nrm(ks[1], (DEC_BATCH, DEC_SEQ, D_MODEL)),
        "cache_k": nrm(ks[2], (DEPTH, n_pool, PAGE_SIZE, N_HEADS, HEAD_DIM)),
        "cache_v": nrm(ks[3], (DEPTH, n_pool, PAGE_SIZE, N_HEADS, HEAD_DIM)),
        "cache_logf": jax.nn.log_sigmoid(head_bias + nrm(ks[4], (DEPTH, n_pool, PAGE_SIZE, N_HEADS))),
        "state_pool": nrm(ks[5], (DEPTH, DEC_BATCH, POOL_HIST, POOL_WIDTH)),
        "page_table": page_table,
        "meta_tokens": nrm(ks[7], (N_META, D_MODEL)),
        "norm1_g": 1.0 + nrm(ks[8], (DEPTH, D_MODEL), 0.1),
        "w_in": nrm(ks[9], (DEPTH, D_MODEL, D_IN_PROJ), D_MODEL ** -0.5),
        "b_forget": head_bias + nrm(ks[10], (DEPTH, N_HEADS), 0.1),
        "pool_w": nrm(ks[11], (DEPTH, POOL_GROUPS, POOL_GROUP_WIDTH, POOL_GROUP_WIDTH), POOL_GROUP_WIDTH ** -0.5),
        "pool_scale": 1.0 + nrm(ks[12], (DEPTH, POOL_WIDTH), 0.1),
        "w_branch_attn": nrm(ks[13], (DEPTH, ATTN_WIDTH, D_MODEL), ATTN_WIDTH ** -0.5),
        "w_branch_pool": nrm(ks[14], (DEPTH, POOL_WIDTH, D_MODEL), POOL_WIDTH ** -0.5),
        "w_out": nrm(ks[15], (DEPTH, D_MODEL, D_MODEL), D_MODEL ** -0.5),
        "norm2_g": 1.0 + nrm(ks[16], (DEPTH, D_MODEL), 0.1),
        "w_router_group": nrm(ks[17], (DEPTH, D_MODEL, N_EXPERT_GROUPS), D_MODEL ** -0.5),
        "b_router_group": nrm(ks[18], (DEPTH, N_EXPERT_GROUPS), 0.01),
        "w_router_expert": nrm(ks[19], (DEPTH, D_MODEL, N_EXPERTS), D_MODEL ** -0.5),
        "b_router_expert": nrm(ks[20], (DEPTH, N_EXPERTS), 0.01),
        "w_gate": nrm(ks[21], (DEPTH, N_EXPERTS, D_MODEL, D_EXPERT), D_MODEL ** -0.5),
        "w_up": nrm(ks[22], (DEPTH, N_EXPERTS, D_MODEL, D_EXPERT), D_MODEL ** -0.5),
        "w_down": nrm(ks[23], (DEPTH, N_EXPERTS, D_EXPERT, D_MODEL), D_EXPERT ** -0.5),
        "norm_f_g": 1.0 + nrm(ks[24], (D_MODEL,), 0.1),
    }


def reference(x_prompt, x_sample, cache_k, cache_v, cache_logf, state_pool, page_table, meta_tokens,
              norm1_g, w_in, b_forget, pool_w, pool_scale, w_branch_attn, w_branch_pool, w_out,
              norm2_g, w_router_group, b_router_group, w_router_expert, b_router_expert,
              w_gate, w_up, w_down, norm_f_g):
    B = x_prompt.shape[0]
    meta = jnp.broadcast_to(meta_tokens.astype(x_prompt.dtype)[None], (B, N_META, D_MODEL))
    hp = jnp.concatenate([meta, x_prompt], axis=1)
    hs = x_sample
    kp_l, vp_l, lfp_l, pp_l, ks_l, vs_l, lfs_l, ps_l = [], [], [], [], [], [], [], []
    for l in range(DEPTH):
        q, k, v, logf, u, ga, gb = mixer_inputs(hp, norm1_g[l], w_in[l], b_forget[l])
        a = fox_prompt(q, k, v, logf)
        hist0 = jnp.zeros((B, POOL_HIST, POOL_WIDTH), u.dtype)
        p = multiscale_pool(u, hist0, 0, pool_w[l], pool_scale[l])
        hp = merge_branches(hp, a, p, ga, gb, w_branch_attn[l], w_branch_pool[l], w_out[l])
        hp = ffn_block(hp, norm2_g[l], w_router_group[l], b_router_group[l], w_router_expert[l],
                       b_router_expert[l], w_gate[l], w_up[l], w_down[l])
        kp_l.append(k); vp_l.append(v); lfp_l.append(logf); pp_l.append(u[:, -POOL_HIST:])
        q, k, v, logf, u, ga, gb = mixer_inputs(hs, norm1_g[l], w_in[l], b_forget[l])
        a = fox_sample(q, k, v, logf, cache_k, cache_v, cache_logf, l, page_table)
        hist = state_pool[l]
        p = multiscale_pool(u, hist, PAST_LEN, pool_w[l], pool_scale[l])
        hs = merge_branches(hs, a, p, ga, gb, w_branch_attn[l], w_branch_pool[l], w_out[l])
        hs = ffn_block(hs, norm2_g[l], w_router_group[l], b_router_group[l], w_router_expert[l],
                       b_router_expert[l], w_gate[l], w_up[l], w_down[l])
        ks_l.append(k); vs_l.append(v); lfs_l.append(logf)
        ps_l.append(jnp.concatenate([hist.astype(u.dtype), u], axis=1)[:, -POOL_HIST:])
    y_prompt = rmsnorm(hp, norm_f_g)[:, N_META:]
    y_sample = rmsnorm(hs, norm_f_g)
    return (y_prompt, y_sample,
            jnp.stack(kp_l), jnp.stack(vp_l), jnp.stack(lfp_l), jnp.stack(pp_l),
            jnp.stack(ks_l), jnp.stack(vs_l), jnp.stack(lfs_l), jnp.stack(ps_l))
```

```python
import functools

import jax
import jax.numpy as jnp
from jax import lax
from jax.experimental import pallas as pl
from jax.experimental.pallas import tpu as pltpu

F32 = jnp.float32
BF16 = jnp.bfloat16

N_HEADS = 8
HEAD_DIM = 64
N_META = 16
POOL_WINDOWS = (2, 4, 8, 16)
POOL_HIST = 15
N_EXPERT_GROUPS = 4
EXPERTS_PER_GROUP = 4
N_EXPERTS = 16
RMS_EPS = 1e-6

LANES = 128
TM = 512
BQ = 512
BK = 512
PAGES_PER_STEP = 16
NEG_BIG = -1e30
VMEM_LIMIT = 52 * 1024 * 1024


def _rmsnorm(x, g):
    inv = lax.rsqrt(jnp.mean(x * x, axis=-1, keepdims=True) + RMS_EPS)
    return (x * inv) * g


def _log_sigmoid(x):
    return jnp.minimum(x, 0.0) - jnp.log1p(jnp.exp(-jnp.abs(x)))


def _split3(x):
    hi = x.astype(BF16)
    r1 = x - hi.astype(F32)
    mid = r1.astype(BF16)
    lo = (r1 - mid.astype(F32)).astype(BF16)
    return hi, mid, lo


def _dot(a, b):
    return jnp.dot(a, b, preferred_element_type=F32)


def _dot3(parts, m):
    return _dot(parts[0], m) + _dot(parts[1], m) + _dot(parts[2], m)


def _dot3_left(m, parts):
    return _dot(m, parts[0]) + _dot(m, parts[1]) + _dot(m, parts[2])


def _inproj_common(x_ref, g_ref, w_ref, bf_ref):
    aw = N_HEADS * HEAD_DIM
    xn = _rmsnorm(x_ref[...], g_ref[...]).astype(BF16)
    proj = _dot(xn, w_ref[...])
    q = proj[:, 0:aw]
    k = proj[:, aw:2 * aw]
    v = proj[:, 2 * aw:3 * aw]
    lf = _log_sigmoid(proj[:, 3 * aw:3 * aw + LANES] + bf_ref[...])
    return q, k, v, lf


def _inproj_prompt_kernel(x_ref, g_ref, w_ref, bf_ref, tri_ref, place_ref,
                          k_ref, v_ref, lf_ref, qt_ref, kaug_ref, vt_ref, tot_ref):
    q, k, v, lf = _inproj_common(x_ref, g_ref, w_ref, bf_ref)
    k_ref[...] = k
    v_ref[...] = v
    lf_ref[...] = lf[:, 0:N_HEADS]

    cs = _dot3_left(tri_ref[...], _split3(lf))
    tot_ref[0] = cs[TM - 1:TM, :]

    c3 = _split3(cs)
    aug = _dot(c3[0], place_ref[0]) + _dot(c3[1], place_ref[1]) + _dot(c3[2], place_ref[2])

    lane = lax.broadcasted_iota(jnp.int32, (TM, LANES), 1)
    low = lane < HEAD_DIM
    qt = (q * (HEAD_DIM ** -0.5)).T
    vt = v.T
    row = lax.broadcasted_iota(jnp.int32, (HEAD_DIM, TM), 0)
    minus_ones = jnp.where(row < 3, -1.0, 0.0).astype(BF16)
    for h in range(N_HEADS):
        kp = k[:, (h // 2) * LANES:(h // 2 + 1) * LANES]
        own = low if h % 2 == 0 else jnp.logical_not(low)
        kaug_ref[h] = (jnp.where(own, kp, 0.0) + aug[:, h * LANES:(h + 1) * LANES]).astype(BF16)
        qh = qt[h * HEAD_DIM:(h + 1) * HEAD_DIM, :].astype(BF16)
        if h % 2 == 0:
            qt_ref[h, 0:HEAD_DIM, :] = qh
            qt_ref[h, HEAD_DIM:2 * HEAD_DIM, :] = minus_ones
        else:
            qt_ref[h, 0:HEAD_DIM, :] = minus_ones
            qt_ref[h, HEAD_DIM:2 * HEAD_DIM, :] = qh
        vt_ref[h] = vt[h * HEAD_DIM:(h + 1) * HEAD_DIM, :].astype(BF16)


def _inproj_sample_kernel(x_ref, g_ref, w_ref, bf_ref, q_ref, k_ref, v_ref, lf_ref, lft_ref):
    q, k, v, lf = _inproj_common(x_ref, g_ref, w_ref, bf_ref)
    q_ref[...] = q
    k_ref[...] = k
    v_ref[...] = v
    lf_ref[...] = lf[:, 0:N_HEADS]
    lft_ref[...] = lf.T[0:N_HEADS, :]


def _full(shape):
    n = len(shape)
    return pl.BlockSpec(shape, lambda *_: (0,) * n)


def _inproj_prompt(hp, g1, w_qkvf, b_f, tri, place, seq_len):
    tpad = hp.shape[0]
    nt = tpad // TM
    d = hp.shape[1]
    aw = N_HEADS * HEAD_DIM
    row = lambda i: (i, 0)
    return pl.pallas_call(
        _inproj_prompt_kernel,
        grid=(nt,),
        in_specs=[pl.BlockSpec((TM, d), row), _full(g1.shape), _full(w_qkvf.shape), _full(b_f.shape),
                  _full(tri.shape), _full(place.shape)],
        out_specs=[pl.BlockSpec((TM, aw), row), pl.BlockSpec((TM, aw), row), pl.BlockSpec((TM, N_HEADS), row),
                   pl.BlockSpec((N_HEADS, LANES, TM), lambda i: (0, 0, i)),
                   pl.BlockSpec((N_HEADS, TM, LANES), lambda i: (0, i, 0)),
                   pl.BlockSpec((N_HEADS, HEAD_DIM, TM), lambda i: (0, 0, i)),
                   pl.BlockSpec((1, 1, LANES), lambda i: (i, 0, 0))],
        out_shape=[jax.ShapeDtypeStruct((seq_len, aw), F32), jax.ShapeDtypeStruct((seq_len, aw), F32),
                   jax.ShapeDtypeStruct((seq_len, N_HEADS), F32),
                   jax.ShapeDtypeStruct((N_HEADS, LANES, tpad), BF16),
                   jax.ShapeDtypeStruct((N_HEADS, tpad, LANES), BF16),
                   jax.ShapeDtypeStruct((N_HEADS, HEAD_DIM, tpad), BF16),
                   jax.ShapeDtypeStruct((nt, 1, LANES), F32)],
        compiler_params=pltpu.CompilerParams(dimension_semantics=("arbitrary",), vmem_limit_bytes=VMEM_LIMIT),
        name="inproj_prompt",
    )(hp, g1, w_qkvf, b_f, tri, place)


def _inproj_sample(xs, g1, w_qkvf, b_f):
    n, d = xs.shape
    nt = n // TM
    aw = N_HEADS * HEAD_DIM
    row = lambda i: (i, 0)
    return pl.pallas_call(
        _inproj_sample_kernel,
        grid=(nt,),
        in_specs=[pl.BlockSpec((TM, d), row), _full(g1.shape), _full(w_qkvf.shape), _full(b_f.shape)],
        out_specs=[pl.BlockSpec((TM, aw), row)] * 3 + [pl.BlockSpec((TM, N_HEADS), row),
                                                        pl.BlockSpec((N_HEADS, TM), lambda i: (0, i))],
        out_shape=[jax.ShapeDtypeStruct((n, aw), F32)] * 3 + [jax.ShapeDtypeStruct((n, N_HEADS), F32),
                                                               jax.ShapeDtypeStruct((N_HEADS, n), F32)],
        compiler_params=pltpu.CompilerParams(dimension_semantics=("arbitrary",), vmem_limit_bytes=VMEM_LIMIT),
        name="inproj_sample",
    )(xs, g1, w_qkvf, b_f)


def _flash_kernel(qi_ref, kj_ref, tot_ref, qt_ref, kaug_ref, vt_ref, o_ref, m_ref, l_ref, acc_ref):
    p = pl.program_id(0)
    i = qi_ref[p]
    j = kj_ref[p]

    @pl.when(j == 0)
    def _():
        m_ref[...] = jnp.full(m_ref.shape, NEG_BIG, F32)
        l_ref[...] = jnp.zeros(l_ref.shape, F32)
        acc_ref[...] = jnp.zeros(acc_ref.shape, F32)

    def block(masked):
        if masked:
            s_pos = lax.broadcasted_iota(jnp.int32, (BK, BQ), 0)
            t_pos = lax.broadcasted_iota(jnp.int32, (BK, BQ), 1)
            visible = s_pos <= t_pos
        for h in range(N_HEADS):
            z = _dot(kaug_ref[h], qt_ref[h])
            if masked:
                z = jnp.where(visible, z, NEG_BIG)
            m_prev = m_ref[h]
            m_loc = jnp.maximum(m_prev, jnp.max(z, axis=0, keepdims=True))
            pexp = jnp.exp(z - m_loc)
            alpha = jnp.exp(m_prev - m_loc)
            l_ref[h] = alpha * l_ref[h] + jnp.sum(pexp, axis=0, keepdims=True)
            rows = slice(h * HEAD_DIM, (h + 1) * HEAD_DIM)
            acc_ref[rows, :] = alpha * acc_ref[rows, :] + _dot(vt_ref[h], pexp.astype(BF16))
            m_ref[h] = m_loc + tot_ref[j * N_HEADS + h]

    @pl.when(j < i)
    def _():
        block(False)

    @pl.when(j == i)
    def _():
        block(True)
        for h in range(N_HEADS):
            rows = slice(h * HEAD_DIM, (h + 1) * HEAD_DIM)
            acc_ref[rows, :] = acc_ref[rows, :] / l_ref[h]
        o_ref[...] = acc_ref[...].T.astype(BF16)


def _flash(qt, kaug, vt, tot_flat):
    tpad = qt.shape[2]
    nq = tpad // BQ
    pairs = [(i, j) for i in range(nq) for j in range(i + 1)]
    qi = jnp.asarray([p[0] for p in pairs], jnp.int32)
    kj = jnp.asarray([p[1] for p in pairs], jnp.int32)
    aw = N_HEADS * HEAD_DIM
    grid_spec = pltpu.PrefetchScalarGridSpec(
        num_scalar_prefetch=3,
        grid=(len(pairs),),
        in_specs=[pl.BlockSpec((N_HEADS, LANES, BQ), lambda p, qi, kj, tot: (0, 0, qi[p])),
                  pl.BlockSpec((N_HEADS, BK, LANES), lambda p, qi, kj, tot: (0, kj[p], 0)),
                  pl.BlockSpec((N_HEADS, HEAD_DIM, BK), lambda p, qi, kj, tot: (0, 0, kj[p]))],
        out_specs=pl.BlockSpec((BQ, aw), lambda p, qi, kj, tot: (qi[p], 0)),
        scratch_shapes=[pltpu.VMEM((N_HEADS, 1, BQ), F32), pltpu.VMEM((N_HEADS, 1, BQ), F32),
                        pltpu.VMEM((aw, BQ), F32)],
    )
    return pl.pallas_call(
        _flash_kernel,
        grid_spec=grid_spec,
        out_shape=jax.ShapeDtypeStruct((tpad, aw), BF16),
        compiler_params=pltpu.CompilerParams(dimension_semantics=("arbitrary",), vmem_limit_bytes=VMEM_LIMIT),
        name="flash_prompt",
    )(qi, kj, tot_flat, qt, kaug, vt)


SUFFIX_ROWS = 2048


def _suffix_kernel(x_ref, m_ref, o_ref):
    o_ref[...] = _dot3(_split3(x_ref[...]), m_ref[...])


def _suffix(lf_rows, mat):
    n = lf_rows.shape[0]
    return pl.pallas_call(
        _suffix_kernel,
        grid=(n // SUFFIX_ROWS,),
        in_specs=[pl.BlockSpec((SUFFIX_ROWS, LANES), lambda i: (i, 0)), _full(mat.shape)],
        out_specs=pl.BlockSpec((SUFFIX_ROWS, 2 * LANES), lambda i: (i, 0)),
        out_shape=jax.ShapeDtypeStruct((n, 2 * LANES), F32),
        compiler_params=pltpu.CompilerParams(dimension_semantics=("arbitrary",), vmem_limit_bytes=VMEM_LIMIT),
        name="suffix_pages",
    )(lf_rows, mat)


def _decode_kernel(pt_ref, q_ref, kn_ref, vn_ref, lft_ref, tri_ref, *rest, n_pages, n_chunks, page):
    ppc = n_pages // n_chunks
    k_refs = rest[0:ppc]
    v_refs = rest[ppc:2 * ppc]
    s_refs = rest[2 * ppc:3 * ppc]
    o_ref = rest[3 * ppc]
    qbd_ref, m_ref, l_ref, acc_ref, carry_ref = rest[3 * ppc + 1:]
    c = pl.program_id(1)
    t_new = q_ref.shape[0]
    aw = N_HEADS * HEAD_DIM
    n_rows = t_new * N_HEADS

    lane_head = lax.broadcasted_iota(jnp.int32, (N_HEADS, aw), 1) // HEAD_DIM
    head_mask = lane_head == lax.broadcasted_iota(jnp.int32, (N_HEADS, aw), 0)

    def attend(z, values):
        m_prev = m_ref[...]
        m_new = jnp.maximum(m_prev, jnp.max(z, axis=-1, keepdims=True))
        pexp = jnp.exp(z - m_new)
        alpha = jnp.exp(m_prev - m_new)
        l_ref[...] = alpha * l_ref[...] + jnp.sum(pexp, axis=-1, keepdims=True)
        pb = pexp.astype(BF16)
        upd = None
        off = 0
        for val in values:
            n_i = val.shape[0]
            d = _dot(pb[:, off:off + n_i], val)
            upd = d if upd is None else upd + d
            off += n_i
        acc_ref[...] = alpha * acc_ref[...] + upd
        m_ref[...] = m_new

    @pl.when(c == 0)
    def _():
        q = q_ref[...] * (HEAD_DIM ** -0.5)
        qbd = jnp.concatenate(
            [jnp.where(head_mask, jnp.broadcast_to(q[t:t + 1, :], (N_HEADS, aw)), 0.0) for t in range(t_new)], axis=0)
        qbd_ref[...] = qbd.astype(BF16)
        m_ref[...] = jnp.full(m_ref.shape, NEG_BIG, F32)
        l_ref[...] = jnp.zeros(l_ref.shape, F32)
        acc_ref[...] = jnp.zeros(acc_ref.shape, F32)
        carry_ref[...] = jnp.zeros(carry_ref.shape, F32)

        cnew = _dot3(_split3(lft_ref[...]), tri_ref[...])
        kn = kn_ref[...].astype(BF16)
        z = lax.dot_general(qbd_ref[...], kn, (((1,), (1,)), ((), ())), preferred_element_type=F32)
        z = z - jnp.concatenate([cnew] * t_new, axis=0)
        t_of_row = lax.broadcasted_iota(jnp.int32, (n_rows, LANES), 0) // N_HEADS
        key = lax.broadcasted_iota(jnp.int32, (n_rows, LANES), 1)
        z = jnp.where(key <= t_of_row, z, NEG_BIG)
        attend(z, [vn_ref[...].astype(BF16)])

    qbd = qbd_ref[...]
    carry = carry_ref[...]
    zs = []
    vals = []
    for jj in range(ppc):
        kb = k_refs[jj][...].astype(BF16)
        z = lax.dot_general(qbd, kb, (((1,), (1,)), ((), ())), preferred_element_type=F32)
        sfx = s_refs[jj][...]
        bias = sfx[:, 0:page] + carry
        carry = carry + sfx[:, LANES:LANES + page]
        zs.append(z + jnp.concatenate([bias] * t_new, axis=0))
        vals.append(v_refs[jj][...].astype(BF16))
    carry_ref[...] = carry
    attend(jnp.concatenate(zs, axis=-1), vals)

    @pl.when(c == n_chunks - 1)
    def _():
        a = acc_ref[...] / l_ref[...]
        outs = []
        for t in range(t_new):
            blk = jnp.where(head_mask, a[t * N_HEADS:(t + 1) * N_HEADS, :], 0.0)
            outs.append(jnp.sum(blk, axis=0, keepdims=True))
        o_ref[...] = jnp.concatenate(outs, axis=0).astype(o_ref.dtype)


def _decode(page_table, q_s, kn_pad, vn_pad, lft_pad, tri_new, cache_k2, cache_v2, sfx_pages):
    db, t_new, aw = q_s.shape
    n_pages = page_table.shape[1]
    page = cache_k2.shape[1]
    ppc = min(PAGES_PER_STEP, n_pages)
    n_chunks = n_pages // ppc
    pt_flat = page_table.reshape(-1)

    def page_map(jj):
        return lambda b, c, pt: (pt[b * n_pages + n_pages - 1 - (c * ppc + jj)], 0, 0)

    seq3 = lambda b, c, pt: (b, 0, 0)
    in_specs = [pl.BlockSpec((None, t_new, aw), seq3),
                pl.BlockSpec((None, LANES, aw), seq3),
                pl.BlockSpec((None, LANES, aw), seq3),
                pl.BlockSpec((None, N_HEADS, LANES), seq3),
                pl.BlockSpec(tri_new.shape, lambda b, c, pt: (0, 0))]
    in_specs += [pl.BlockSpec((None, page, aw), page_map(jj)) for jj in range(ppc)]
    in_specs += [pl.BlockSpec((None, page, aw), page_map(jj)) for jj in range(ppc)]
    in_specs += [pl.BlockSpec((None, N_HEADS, 2 * LANES), page_map(jj)) for jj in range(ppc)]
    n_rows = t_new * N_HEADS
    grid_spec = pltpu.PrefetchScalarGridSpec(
        num_scalar_prefetch=1,
        grid=(db, n_chunks),
        in_specs=in_specs,
        out_specs=pl.BlockSpec((None, t_new, aw), seq3),
        scratch_shapes=[pltpu.VMEM((n_rows, aw), BF16), pltpu.VMEM((n_rows, 1), F32), pltpu.VMEM((n_rows, 1), F32),
                        pltpu.VMEM((n_rows, aw), F32), pltpu.VMEM((N_HEADS, LANES), F32)],
    )
    kern = functools.partial(_decode_kernel, n_pages=n_pages, n_chunks=n_chunks, page=page)
    return pl.pallas_call(
        kern,
        grid_spec=grid_spec,
        out_shape=jax.ShapeDtypeStruct((db, t_new, aw), BF16),
        compiler_params=pltpu.CompilerParams(dimension_semantics=("arbitrary", "arbitrary"),
                                             vmem_limit_bytes=VMEM_LIMIT),
        name="decode_paged",
    )(pt_flat, q_s, kn_pad, vn_pad, lft_pad, tri_new, *([cache_k2] * ppc), *([cache_v2] * ppc), *([sfx_pages] * ppc))


def _mix_and_route(h, xn, a_bf, d_groups, w_ugg_ga_gb, poolw_ref, pscale_ref, wba_ref, wbp_ref, wout_ref,
                   g2_ref, wrt_ref, br_ref, h1_ref, xn2_ref, comb_ref):
    ga, gb = w_ugg_ga_gb
    mixed = [_dot(d_groups[g].astype(BF16), poolw_ref[g]) for g in range(len(POOL_WINDOWS))]
    pooled = jnp.concatenate(mixed, axis=-1) * pscale_ref[...]
    m = jax.nn.sigmoid(ga) * _dot(a_bf, wba_ref[...]) + jax.nn.sigmoid(gb) * _dot(pooled.astype(BF16), wbp_ref[...])
    h1 = h + _dot(m.astype(BF16), wout_ref[...])
    h1_ref[...] = h1
    xn2 = _rmsnorm(h1, g2_ref[...])
    xn2_bf = xn2.astype(BF16)
    xn2_ref[...] = xn2_bf

    lt = lax.dot_general(wrt_ref[...], xn2_bf, (((1,), (1,)), ((), ())), preferred_element_type=F32) + br_ref[...]
    g_rows = [lt[g:g + 1, :] for g in range(N_EXPERT_GROUPS)]
    gmax = functools.reduce(jnp.maximum, g_rows)
    gsum = functools.reduce(lambda a, b: a + b, [jnp.exp(r - gmax) for r in g_rows])
    pg_sel = 1.0 / gsum
    gsel = jnp.full(gmax.shape, N_EXPERT_GROUPS - 1, jnp.int32)
    for g in range(N_EXPERT_GROUPS - 2, -1, -1):
        gsel = jnp.where(g_rows[g] == gmax, g, gsel)
    e_rows = []
    for k in range(EXPERTS_PER_GROUP):
        r = lt[N_EXPERT_GROUPS + k:N_EXPERT_GROUPS + k + 1, :]
        for g in range(1, N_EXPERT_GROUPS):
            base = N_EXPERT_GROUPS + g * EXPERTS_PER_GROUP + k
            r = jnp.where(gsel == g, lt[base:base + 1, :], r)
        e_rows.append(r)
    v1 = functools.reduce(jnp.maximum, e_rows)
    i1 = jnp.full(gsel.shape, EXPERTS_PER_GROUP - 1, jnp.int32)
    for k in range(EXPERTS_PER_GROUP - 2, -1, -1):
        i1 = jnp.where(e_rows[k] == v1, k, i1)
    rest = [jnp.where(i1 == k, -jnp.inf, e_rows[k]) for k in range(EXPERTS_PER_GROUP)]
    v2 = functools.reduce(jnp.maximum, rest)
    i2 = jnp.full(gsel.shape, EXPERTS_PER_GROUP - 1, jnp.int32)
    for k in range(EXPERTS_PER_GROUP - 2, -1, -1):
        i2 = jnp.where((rest[k] == v2) & (i1 != k), k, i2)
    e2 = jnp.exp(v2 - v1)
    w1 = (1.0 / (1.0 + e2)) * pg_sel
    w2 = (e2 / (1.0 + e2)) * pg_sel
    id1 = gsel * EXPERTS_PER_GROUP + i1
    id2 = gsel * EXPERTS_PER_GROUP + i2
    ntok = lt.shape[1]
    erow = lax.broadcasted_iota(jnp.int32, (LANES, ntok), 0)
    comb_t = jnp.where(erow == id1, w1, 0.0) + jnp.where(erow == id2, w2, 0.0)
    comb_ref[...] = comb_t.T


def _merge_prompt_kernel(h_ref, a_ref, g1_ref, wugg_ref, poolw_ref, pscale_ref, wba_ref, wbp_ref, wout_ref,
                         g2_ref, wrt_ref, br_ref, h1_ref, xn2_ref, comb_ref, utail_ref, ext_ref, tail_ref):
    i = pl.program_id(0)
    pw = LANES * len(POOL_WINDOWS)
    hist_rows = 16

    @pl.when(i == 0)
    def _():
        tail_ref[...] = jnp.zeros(tail_ref.shape, F32)

    h = h_ref[...]
    xn = _rmsnorm(h, g1_ref[...]).astype(BF16)
    ugg = _dot(xn, wugg_ref[...])
    d = h.shape[1]
    u = ugg[:, 0:pw]
    ga = ugg[:, pw:pw + d]
    gb = ugg[:, pw + d:pw + 2 * d]
    utail_ref[...] = u[0:hist_rows, :]

    ext_ref[0:hist_rows, :] = tail_ref[...]
    ext_ref[hist_rows:hist_rows + TM, :] = u
    tail_ref[...] = u[TM - hist_rows:TM, :]

    pos = lax.broadcasted_iota(jnp.int32, (TM, LANES), 0) + i * TM
    d_groups = []
    for g, w in enumerate(POOL_WINDOWS):
        lanes = slice(g * LANES, (g + 1) * LANES)
        tok = ext_ref[hist_rows:hist_rows + TM, lanes]
        acc = tok
        for back in range(1, w):
            acc = acc + ext_ref[hist_rows - back:hist_rows - back + TM, lanes]
        cnt = jnp.minimum(pos + 1, w).astype(F32)
        d_groups.append(acc / cnt - tok)

    _mix_and_route(h, xn, a_ref[...], d_groups, (ga, gb), poolw_ref, pscale_ref, wba_ref, wbp_ref, wout_ref,
                   g2_ref, wrt_ref, br_ref, h1_ref, xn2_ref, comb_ref)


def _merge_sample_kernel(h_ref, a_ref, hist_ref, g1_ref, wugg_ref, poolw_ref, pscale_ref, wba_ref, wbp_ref, wout_ref,
                         g2_ref, wrt_ref, br_ref, h1_ref, xn2_ref, comb_ref, u_ref, *, pos0, db):
    pw = LANES * len(POOL_WINDOWS)
    h = h_ref[...]
    xn = _rmsnorm(h, g1_ref[...]).astype(BF16)
    ugg = _dot(xn, wugg_ref[...])
    d = h.shape[1]
    u = ugg[:, 0:pw]
    ga = ugg[:, pw:pw + d]
    gb = ugg[:, pw + d:pw + 2 * d]
    u_ref[...] = u
    t_new = h.shape[0] // db
    n_hist = hist_ref.shape[0]

    def ext(e, lanes):
        if e < n_hist:
            return hist_ref[e][:, lanes]
        return u[(e - n_hist) * db:(e - n_hist + 1) * db, lanes]

    d_groups = []
    for g, w in enumerate(POOL_WINDOWS):
        lanes = slice(g * LANES, (g + 1) * LANES)
        per_t = []
        for t in range(t_new):
            tok = ext(n_hist + t, lanes)
            acc = tok
            for back in range(1, w):
                acc = acc + ext(n_hist + t - back, lanes)
            cnt = float(min(pos0 + t + 1, w))
            per_t.append(acc / cnt - tok)
        d_groups.append(jnp.concatenate(per_t, axis=0))

    _mix_and_route(h, xn, a_ref[...], d_groups, (ga, gb), poolw_ref, pscale_ref, wba_ref, wbp_ref, wout_ref,
                   g2_ref, wrt_ref, br_ref, h1_ref, xn2_ref, comb_ref)


def _merge_weights_specs(ws):
    return [_full(w.shape) for w in ws]


def _merge_prompt(hp, a_bf, ws):
    tpad, d = hp.shape
    nt = tpad // TM
    aw = a_bf.shape[1]
    pw = LANES * len(POOL_WINDOWS)
    row = lambda i: (i, 0)
    return pl.pallas_call(
        _merge_prompt_kernel,
        grid=(nt,),
        in_specs=[pl.BlockSpec((TM, d), row), pl.BlockSpec((TM, aw), row)] + _merge_weights_specs(ws),
        out_specs=[pl.BlockSpec((TM, d), row), pl.BlockSpec((TM, d), row), pl.BlockSpec((TM, LANES), row),
                   pl.BlockSpec((16, pw), lambda i: (0, 0))],
        out_shape=[jax.ShapeDtypeStruct((tpad, d), F32), jax.ShapeDtypeStruct((tpad, d), BF16),
                   jax.ShapeDtypeStruct((tpad, LANES), F32), jax.ShapeDtypeStruct((16, pw), F32)],
        scratch_shapes=[pltpu.VMEM((TM + 16, pw), F32), pltpu.VMEM((16, pw), F32)],
        compiler_params=pltpu.CompilerParams(dimension_semantics=("arbitrary",), vmem_limit_bytes=VMEM_LIMIT),
        name="merge_prompt",
    )(hp, a_bf, *ws)


def _merge_sample(xs, a_bf, hist_t, ws, pos0, db):
    n, d = xs.shape
    assert n == TM
    aw = a_bf.shape[1]
    pw = LANES * len(POOL_WINDOWS)
    row = lambda i: (i, 0)
    kern = functools.partial(_merge_sample_kernel, pos0=pos0, db=db)
    return pl.pallas_call(
        kern,
        grid=(1,),
        in_specs=[pl.BlockSpec((TM, d), row), pl.BlockSpec((TM, aw), row), _full(hist_t.shape)]
        + _merge_weights_specs(ws),
        out_specs=[pl.BlockSpec((TM, d), row), pl.BlockSpec((TM, d), row), pl.BlockSpec((TM, LANES), row),
                   pl.BlockSpec((TM, pw), row)],
        out_shape=[jax.ShapeDtypeStruct((n, d), F32), jax.ShapeDtypeStruct((n, d), BF16),
                   jax.ShapeDtypeStruct((n, LANES), F32), jax.ShapeDtypeStruct((n, pw), F32)],
        compiler_params=pltpu.CompilerParams(dimension_semantics=("arbitrary",), vmem_limit_bytes=VMEM_LIMIT),
        name="merge_sample",
    )(xs, a_bf, hist_t, *ws)


def _moe_kernel(x_ref, comb_ref, h1_ref, wg_ref, wu_ref, wd_ref, gf_ref, o_ref, acc_ref):
    e = pl.program_id(1)

    @pl.when(e == 0)
    def _():
        acc_ref[...] = jnp.zeros(acc_ref.shape, F32)

    x = x_ref[...]
    hdn = jax.nn.silu(_dot(x, wg_ref[...])) * _dot(x, wu_ref[...])
    out = _dot(hdn.astype(BF16), wd_ref[...])
    lane = lax.broadcasted_iota(jnp.int32, comb_ref.shape, 1)
    col = jnp.sum(jnp.where(lane == e, comb_ref[...], 0.0), axis=-1, keepdims=True)
    acc_ref[...] += col * out

    @pl.when(e == pl.num_programs(1) - 1)
    def _():
        o_ref[...] = _rmsnorm(h1_ref[...] + acc_ref[...], gf_ref[...])


def _moe(xn2, comb, h1, wg, wu, wd, gf):
    n, d = xn2.shape
    nt = n // TM
    ne, _, de = wg.shape
    row = lambda i, e: (i, 0)
    return pl.pallas_call(
        _moe_kernel,
        grid=(nt, ne),
        in_specs=[pl.BlockSpec((TM, d), row), pl.BlockSpec((TM, LANES), row), pl.BlockSpec((TM, d), row),
                  pl.BlockSpec((None, d, de), lambda i, e: (e, 0, 0)),
                  pl.BlockSpec((None, d, de), lambda i, e: (e, 0, 0)),
                  pl.BlockSpec((None, de, d), lambda i, e: (e, 0, 0)),
                  pl.BlockSpec(gf.shape, lambda i, e: (0, 0))],
        out_specs=pl.BlockSpec((TM, d), row),
        out_shape=jax.ShapeDtypeStruct((n, d), F32),
        scratch_shapes=[pltpu.VMEM((TM, d), F32)],
        compiler_params=pltpu.CompilerParams(dimension_semantics=("arbitrary", "arbitrary"),
                                             vmem_limit_bytes=VMEM_LIMIT),
        name="moe_dense",
    )(xn2, comb, h1, wg, wu, wd, gf)


def _placement_matrices():
    import numpy as np
    m = np.zeros((3, LANES, N_HEADS * LANES), np.float32)
    for h in range(N_HEADS):
        base = h * LANES + (HEAD_DIM if h % 2 == 0 else 0)
        for j in range(3):
            m[j, h, base + j] = 1.0
    return jnp.asarray(m, BF16)


def kernel(x_prompt, x_sample, cache_k, cache_v, cache_logf, state_pool, page_table, meta_tokens, norm1_g, w_in,
           b_forget, pool_w, pool_scale, w_branch_attn, w_branch_pool, w_out, norm2_g, w_router_group,
           b_router_group, w_router_expert, b_router_expert, w_gate, w_up, w_down, norm_f_g):
    import numpy as np
    depth = w_in.shape[0]
    assert depth == 1
    batch, seq, d = x_prompt.shape
    assert batch == 1
    db, t_new, _ = x_sample.shape
    assert db * t_new == TM
    aw = N_HEADS * HEAD_DIM
    pw = LANES * len(POOL_WINDOWS)
    seq_len = seq + N_META
    tpad = -(-seq_len // TM) * TM
    assert (seq_len - 16) % TM == 0 and tpad > seq_len
    n_pool, page = cache_k.shape[1], cache_k.shape[2]
    n_pages = page_table.shape[1]
    past_len = n_pages * page
    assert page == LANES and t_new <= 8

    wl = w_in[0]
    w_qkvf = jnp.concatenate([wl[:, 0:3 * aw], jnp.pad(wl[:, 3 * aw:3 * aw + N_HEADS], ((0, 0), (0, LANES - N_HEADS)))],
                             axis=1).astype(BF16)
    w_ugg = wl[:, 3 * aw + N_HEADS:].astype(BF16)
    b_f = jnp.pad(b_forget[0], (0, LANES - N_HEADS)).reshape(1, LANES)
    g1 = norm1_g[0].reshape(1, d)
    g2 = norm2_g[0].reshape(1, d)
    gf = norm_f_g.reshape(1, d)
    n_r = N_EXPERT_GROUPS + N_EXPERTS
    w_rt = jnp.pad(jnp.concatenate([w_router_group[0], w_router_expert[0]], axis=1).T, ((0, 32 - n_r), (0, 0))).astype(BF16)
    b_r = jnp.pad(jnp.concatenate([b_router_group[0], b_router_expert[0]]), (0, 32 - n_r))
    b_r = jnp.broadcast_to(b_r[:, None], (32, TM))
    merge_ws = [g1, w_ugg, pool_w[0].astype(BF16), pool_scale[0].reshape(1, pw), w_branch_attn[0].astype(BF16),
                w_branch_pool[0].astype(BF16), w_out[0].astype(BF16), g2, w_rt, b_r]
    wg = w_gate[0].astype(BF16)
    wu = w_up[0].astype(BF16)
    wd = w_down[0].astype(BF16)

    tri_tm = jnp.asarray(np.tril(np.ones((TM, TM), np.float32)), BF16)
    place = _placement_matrices()

    hp = jnp.concatenate([meta_tokens.astype(x_prompt.dtype), x_prompt[0],
                          jnp.zeros((tpad - seq_len, d), x_prompt.dtype)], axis=0)
    k_p, v_p, lf_p, qt, kaug, vt, tot = _inproj_prompt(hp, g1, w_qkvf, b_f, tri_tm, place, seq_len)
    a_p = _flash(qt, kaug, vt, tot[:, 0, 0:N_HEADS].reshape(-1))
    h1_p, xn2_p, comb_p, utail = _merge_prompt(hp, a_p, merge_ws)
    y_p = _moe(xn2_p, comb_p, h1_p, wg, wu, wd, gf)

    xs = jnp.transpose(x_sample, (1, 0, 2)).reshape(TM, d)
    q_s, k_s, v_s, lf_s, lft_s = _inproj_sample(xs, g1, w_qkvf, b_f)
    to_seq = lambda z: jnp.transpose(z.reshape(t_new, db, -1), (1, 0, 2))
    k_seq, v_seq = to_seq(k_s), to_seq(v_s)
    pad_rows = lambda z: jnp.pad(z, ((0, 0), (0, LANES - t_new), (0, 0)))
    lft_pad = jnp.pad(jnp.transpose(lft_s.reshape(N_HEADS, t_new, db), (2, 0, 1)), ((0, 0), (0, 0), (0, LANES - t_new)))
    tri_new = jnp.asarray(np.triu(np.ones((LANES, LANES), np.float32)) * (np.arange(LANES) < t_new)[None, :], BF16)

    lf_rows = jnp.transpose(cache_logf[0], (0, 2, 1)).reshape(n_pool * N_HEADS, page)
    sfx_mat = jnp.asarray(np.concatenate([np.tril(np.ones((page, page), np.float32), -1),
                                          np.ones((page, page), np.float32)], axis=1), BF16)
    rows_pad = -(-lf_rows.shape[0] // SUFFIX_ROWS) * SUFFIX_ROWS
    sfx = _suffix(jnp.pad(lf_rows, ((0, rows_pad - lf_rows.shape[0]), (0, 0))), sfx_mat)
    sfx_pages = sfx[:n_pool * N_HEADS].reshape(n_pool, N_HEADS, 2 * LANES)

    a_s = _decode(page_table, to_seq(q_s), pad_rows(k_seq), pad_rows(v_seq), lft_pad, tri_new,
                  cache_k[0].reshape(n_pool, page, aw), cache_v[0].reshape(n_pool, page, aw), sfx_pages)
    a_s_t = jnp.transpose(a_s, (1, 0, 2)).reshape(TM, aw)
    hist_t = jnp.transpose(state_pool[0], (1, 0, 2))
    h1_s, xn2_s, comb_s, u_s = _merge_sample(xs, a_s_t, hist_t, merge_ws, past_len, db)
    y_s = _moe(xn2_s, comb_s, h1_s, wg, wu, wd, gf)

    y_prompt = y_p[N_META:seq_len].reshape(1, seq, d)
    y_sample = to_seq(y_s)
    k_prompt = k_p.reshape(1, 1, seq_len, N_HEADS, HEAD_DIM)
    v_prompt = v_p.reshape(1, 1, seq_len, N_HEADS, HEAD_DIM)
    logf_prompt = lf_p.reshape(1, 1, seq_len, N_HEADS)
    pool_prompt = utail[16 - POOL_HIST:16].reshape(1, 1, POOL_HIST, pw)
    k_sample = k_seq.reshape(1, db, t_new, N_HEADS, HEAD_DIM)
    v_sample = v_seq.reshape(1, db, t_new, N_HEADS, HEAD_DIM)
    logf_sample = to_seq(lf_s).reshape(1, db, t_new, N_HEADS)
    pool_sample = jnp.concatenate([state_pool[0].astype(F32), to_seq(u_s)], axis=1)[:, -POOL_HIST:].reshape(
        1, db, POOL_HIST, pw)
    return (y_prompt, y_sample, k_prompt, v_prompt, logf_prompt, pool_prompt,
            k_sample, v_sample, logf_sample, pool_sample)
```

```python
import functools

import jax
import jax.numpy as jnp
from jax import lax
from jax.experimental import pallas as pl
from jax.experimental.pallas import tpu as pltpu

F32 = jnp.float32
BF16 = jnp.bfloat16

N_HEADS = 8
HEAD_DIM = 64
N_META = 16
POOL_WINDOWS = (2, 4, 8, 16)
POOL_HIST = 15
N_EXPERT_GROUPS = 4
EXPERTS_PER_GROUP = 4
N_EXPERTS = 16
RMS_EPS = 1e-6

LANES = 128
TM = 512
BQ = 512
BK = 512
PAGES_PER_STEP = 16
SCORE_SLOTS = 3
VROWS = HEAD_DIM + 16
NEG_BIG = -1e30
LOG2E = 1.4426950408889634
VMEM_LIMIT = 52 * 1024 * 1024


def _rmsnorm(x, g):
    inv = lax.rsqrt(jnp.mean(x * x, axis=-1, keepdims=True) + RMS_EPS)
    return (x * inv) * g


def _log_sigmoid(x):
    return jnp.minimum(x, 0.0) - jnp.log1p(jnp.exp(-jnp.abs(x)))


def _split3(x):
    hi = x.astype(BF16)
    r1 = x - hi.astype(F32)
    mid = r1.astype(BF16)
    lo = (r1 - mid.astype(F32)).astype(BF16)
    return hi, mid, lo


def _dot(a, b):
    return jnp.dot(a, b, preferred_element_type=F32)


def _dot3(parts, m):
    return _dot(parts[0], m) + _dot(parts[1], m) + _dot(parts[2], m)


def _dot3_left(m, parts):
    return _dot(m, parts[0]) + _dot(m, parts[1]) + _dot(m, parts[2])


def _inproj_common(x_ref, g_ref, w_ref, bf_ref):
    aw = N_HEADS * HEAD_DIM
    xn = _rmsnorm(x_ref[...], g_ref[...]).astype(BF16)
    proj = _dot(xn, w_ref[...])
    q = proj[:, 0:aw]
    k = proj[:, aw:2 * aw]
    v = proj[:, 2 * aw:3 * aw]
    lf = _log_sigmoid(proj[:, 3 * aw:3 * aw + LANES] + bf_ref[...])
    return q, k, v, lf


def _inproj_prompt_kernel(x_ref, g_ref, w_ref, bf_ref, tri_ref, place_ref,
                          k_ref, v_ref, lf_ref, qt_ref, kaug_ref, vt_ref, tot_ref):
    q, k, v, lf = _inproj_common(x_ref, g_ref, w_ref, bf_ref)
    k_ref[...] = k
    v_ref[...] = v
    lf_ref[...] = lf[:, 0:N_HEADS]

    cs = _dot3_left(tri_ref[...], _split3(lf)) * LOG2E
    tot_ref[0] = cs[TM - 1:TM, :]

    c3 = _split3(cs)
    aug = _dot(c3[0], place_ref[0]) + _dot(c3[1], place_ref[1]) + _dot(c3[2], place_ref[2])

    lane = lax.broadcasted_iota(jnp.int32, (TM, LANES), 1)
    low = lane < HEAD_DIM
    qt = (q * (HEAD_DIM ** -0.5 * LOG2E)).T
    vt = v.T
    row = lax.broadcasted_iota(jnp.int32, (HEAD_DIM, TM), 0)
    minus_ones = jnp.where(row < 3, -1.0, 0.0).astype(BF16)
    ones_row = jnp.where(lax.broadcasted_iota(jnp.int32, (VROWS - HEAD_DIM, TM), 0) == 0, 1.0, 0.0).astype(BF16)
    for h in range(N_HEADS):
        kp = k[:, (h // 2) * LANES:(h // 2 + 1) * LANES]
        own = low if h % 2 == 0 else jnp.logical_not(low)
        kaug_ref[h] = (jnp.where(own, kp, 0.0) + aug[:, h * LANES:(h + 1) * LANES]).astype(BF16)
        qh = qt[h * HEAD_DIM:(h + 1) * HEAD_DIM, :].astype(BF16)
        if h % 2 == 0:
            qt_ref[h, 0:HEAD_DIM, :] = qh
            qt_ref[h, HEAD_DIM:2 * HEAD_DIM, :] = minus_ones
        else:
            qt_ref[h, 0:HEAD_DIM, :] = minus_ones
            qt_ref[h, HEAD_DIM:2 * HEAD_DIM, :] = qh
        vt_ref[h, 0:HEAD_DIM, :] = vt[h * HEAD_DIM:(h + 1) * HEAD_DIM, :].astype(BF16)
        vt_ref[h, HEAD_DIM:VROWS, :] = ones_row


def _inproj_sample_kernel(x_ref, g_ref, w_ref, bf_ref, q_ref, k_ref, v_ref, lf_ref, lft_ref):
    q, k, v, lf = _inproj_common(x_ref, g_ref, w_ref, bf_ref)
    q_ref[...] = q
    k_ref[...] = k
    v_ref[...] = v
    lf_ref[...] = lf[:, 0:N_HEADS]
    lft_ref[...] = lf.T[0:N_HEADS, :]


def _full(shape):
    n = len(shape)
    return pl.BlockSpec(shape, lambda *_: (0,) * n)


def _inproj_prompt(hp, g1, w_qkvf, b_f, tri, place, seq_len):
    tpad = hp.shape[0]
    nt = tpad // TM
    d = hp.shape[1]
    aw = N_HEADS * HEAD_DIM
    row = lambda i: (i, 0)
    return pl.pallas_call(
        _inproj_prompt_kernel,
        grid=(nt,),
        in_specs=[pl.BlockSpec((TM, d), row), _full(g1.shape), _full(w_qkvf.shape), _full(b_f.shape),
                  _full(tri.shape), _full(place.shape)],
        out_specs=[pl.BlockSpec((TM, aw), row), pl.BlockSpec((TM, aw), row), pl.BlockSpec((TM, N_HEADS), row),
                   pl.BlockSpec((N_HEADS, LANES, TM), lambda i: (0, 0, i)),
                   pl.BlockSpec((N_HEADS, TM, LANES), lambda i: (0, i, 0)),
                   pl.BlockSpec((N_HEADS, VROWS, TM), lambda i: (0, 0, i)),
                   pl.BlockSpec((1, 1, LANES), lambda i: (i, 0, 0))],
        out_shape=[jax.ShapeDtypeStruct((seq_len, aw), F32), jax.ShapeDtypeStruct((seq_len, aw), F32),
                   jax.ShapeDtypeStruct((seq_len, N_HEADS), F32),
                   jax.ShapeDtypeStruct((N_HEADS, LANES, tpad), BF16),
                   jax.ShapeDtypeStruct((N_HEADS, tpad, LANES), BF16),
                   jax.ShapeDtypeStruct((N_HEADS, VROWS, tpad), BF16),
                   jax.ShapeDtypeStruct((nt, 1, LANES), F32)],
        compiler_params=pltpu.CompilerParams(dimension_semantics=("arbitrary",), vmem_limit_bytes=VMEM_LIMIT),
        name="inproj_prompt",
    )(hp, g1, w_qkvf, b_f, tri, place)


def _inproj_sample(xs, g1, w_qkvf, b_f):
    n, d = xs.shape
    nt = n // TM
    aw = N_HEADS * HEAD_DIM
    row = lambda i: (i, 0)
    return pl.pallas_call(
        _inproj_sample_kernel,
        grid=(nt,),
        in_specs=[pl.BlockSpec((TM, d), row), _full(g1.shape), _full(w_qkvf.shape), _full(b_f.shape)],
        out_specs=[pl.BlockSpec((TM, aw), row)] * 3 + [pl.BlockSpec((TM, N_HEADS), row),
                                                        pl.BlockSpec((N_HEADS, TM), lambda i: (0, i))],
        out_shape=[jax.ShapeDtypeStruct((n, aw), F32)] * 3 + [jax.ShapeDtypeStruct((n, N_HEADS), F32),
                                                               jax.ShapeDtypeStruct((N_HEADS, n), F32)],
        compiler_params=pltpu.CompilerParams(dimension_semantics=("arbitrary",), vmem_limit_bytes=VMEM_LIMIT),
        name="inproj_sample",
    )(xs, g1, w_qkvf, b_f)


def _flash_kernel(qi_ref, kj_ref, tot_ref, qt_ref, kaug_ref, vt_ref, o_ref, m_ref, acc_ref, z_ref):
    p = pl.program_id(0)
    i = qi_ref[p]
    j = kj_ref[p]

    @pl.when(j == 0)
    def _():
        m_ref[...] = jnp.full(m_ref.shape, NEG_BIG, F32)
        acc_ref[...] = jnp.zeros(acc_ref.shape, F32)

    def scores(h):
        z_ref[h % SCORE_SLOTS] = _dot(kaug_ref[h], qt_ref[h])

    def softmax_pv(h, diagonal):
        z = z_ref[h % SCORE_SLOTS]
        if diagonal:
            s_pos = lax.broadcasted_iota(jnp.int32, (BK, BQ), 0)
            t_pos = lax.broadcasted_iota(jnp.int32, (BK, BQ), 1)
            z = jnp.where(s_pos <= t_pos, z, NEG_BIG)
        m = m_ref[h]
        m_new = jnp.maximum(m, jnp.max(z, axis=0, keepdims=True))
        pexp = jnp.exp2(z - m_new)
        alpha = jnp.exp2(m - m_new)
        acc_ref[h] = alpha * acc_ref[h] + _dot(vt_ref[h], pexp.astype(BF16))
        m_ref[h] = m_new + tot_ref[j * N_HEADS + h]

    def tile(diagonal):
        for h in range(SCORE_SLOTS - 1):
            scores(h)
        for h in range(N_HEADS):
            if h + SCORE_SLOTS - 1 < N_HEADS:
                scores(h + SCORE_SLOTS - 1)
            softmax_pv(h, diagonal)

    @pl.when(j < i)
    def _():
        tile(False)

    @pl.when(j == i)
    def _():
        tile(True)
        outs = []
        for h in range(N_HEADS):
            a = acc_ref[h]
            outs.append(a[0:HEAD_DIM, :] / a[HEAD_DIM:HEAD_DIM + 1, :])
        o_ref[...] = jnp.concatenate(outs, axis=0).T.astype(BF16)


def _flash(qt, kaug, vt, tot_flat):
    tpad = qt.shape[2]
    nq = tpad // BQ
    pairs = [(i, j) for i in range(nq) for j in range(i + 1)]
    qi = jnp.asarray([p[0] for p in pairs], jnp.int32)
    kj = jnp.asarray([p[1] for p in pairs], jnp.int32)
    aw = N_HEADS * HEAD_DIM
    grid_spec = pltpu.PrefetchScalarGridSpec(
        num_scalar_prefetch=3,
        grid=(len(pairs),),
        in_specs=[pl.BlockSpec((N_HEADS, LANES, BQ), lambda p, qi, kj, tot: (0, 0, qi[p])),
                  pl.BlockSpec((N_HEADS, BK, LANES), lambda p, qi, kj, tot: (0, kj[p], 0)),
                  pl.BlockSpec((N_HEADS, VROWS, BK), lambda p, qi, kj, tot: (0, 0, kj[p]))],
        out_specs=pl.BlockSpec((BQ, aw), lambda p, qi, kj, tot: (qi[p], 0)),
        scratch_shapes=[pltpu.VMEM((N_HEADS, 1, BQ), F32), pltpu.VMEM((N_HEADS, VROWS, BQ), F32),
                        pltpu.VMEM((SCORE_SLOTS, BK, BQ), F32)],
    )
    return pl.pallas_call(
        _flash_kernel,
        grid_spec=grid_spec,
        out_shape=jax.ShapeDtypeStruct((tpad, aw), BF16),
        compiler_params=pltpu.CompilerParams(dimension_semantics=("arbitrary",), vmem_limit_bytes=VMEM_LIMIT),
        name="flash_prompt",
    )(qi, kj, tot_flat, qt, kaug, vt)


SUFFIX_ROWS = 2048


def _suffix_kernel(x_ref, m_ref, o_ref):
    o_ref[...] = _dot3(_split3(x_ref[...]), m_ref[...])


def _suffix(lf_rows, mat):
    n = lf_rows.shape[0]
    return pl.pallas_call(
        _suffix_kernel,
        grid=(n // SUFFIX_ROWS,),
        in_specs=[pl.BlockSpec((SUFFIX_ROWS, LANES), lambda i: (i, 0)), _full(mat.shape)],
        out_specs=pl.BlockSpec((SUFFIX_ROWS, 2 * LANES), lambda i: (i, 0)),
        out_shape=jax.ShapeDtypeStruct((n, 2 * LANES), F32),
        compiler_params=pltpu.CompilerParams(dimension_semantics=("arbitrary",), vmem_limit_bytes=VMEM_LIMIT),
        name="suffix_pages",
    )(lf_rows, mat)


def _decode_kernel(pt_ref, q_ref, kn_ref, vn_ref, lft_ref, tri_ref, *rest, n_pages, n_chunks, page):
    ppc = n_pages // n_chunks
    kt_refs = rest[0:ppc]
    vt_refs = rest[ppc:2 * ppc]
    s_refs = rest[2 * ppc:3 * ppc]
    o_ref = rest[3 * ppc]
    qbd_ref, m_ref, l_ref, acc_ref, carry_ref = rest[3 * ppc + 1:]
    c = pl.program_id(1)
    t_new = q_ref.shape[0]
    aw = N_HEADS * HEAD_DIM
    n_rows = t_new * N_HEADS

    lane_head = lax.broadcasted_iota(jnp.int32, (N_HEADS, aw), 1) // HEAD_DIM
    head_mask = lane_head == lax.broadcasted_iota(jnp.int32, (N_HEADS, aw), 0)

    @pl.when(c == 0)
    def _():
        q = q_ref[...] * (HEAD_DIM ** -0.5)
        qbd = jnp.concatenate(
            [jnp.where(head_mask, jnp.broadcast_to(q[t:t + 1, :], (N_HEADS, aw)), 0.0) for t in range(t_new)], axis=0)
        qbd_ref[...] = qbd.astype(BF16)
        carry_ref[...] = jnp.zeros(carry_ref.shape, F32)

        cnew = _dot3(_split3(lft_ref[...]), tri_ref[...])
        cnew = jnp.concatenate([cnew] * t_new, axis=0)
        t_of_row = lax.broadcasted_iota(jnp.int32, (n_rows, 1), 0) // N_HEADS
        kn = kn_ref[...]
        vn = vn_ref[...]
        zs = []
        for t2 in range(t_new):
            zc = jnp.sum(qbd * kn[t2:t2 + 1, :], axis=-1, keepdims=True)
            zc = zc - cnew[:, t2:t2 + 1]
            zs.append(jnp.where(t_of_row >= t2, zc, NEG_BIG))
        m = functools.reduce(jnp.maximum, zs)
        l = jnp.zeros((n_rows, 1), F32)
        acc = jnp.zeros((n_rows, aw), F32)
        for t2 in range(t_new):
            pe = jnp.exp(zs[t2] - m)
            l = l + pe
            acc = acc + pe * vn[t2:t2 + 1, :]
        m_ref[...] = m
        l_ref[...] = l
        acc_ref[...] = acc

    carry = carry_ref[...]
    biases = []
    for jj in range(ppc):
        sfx = s_refs[jj][...]
        biases.append(sfx[:, 0:page] + carry)
        carry = carry + sfx[:, LANES:LANES + page]
    carry_ref[...] = carry
    bias = jnp.concatenate(biases, axis=1)
    ktc = jnp.concatenate([kt_refs[jj][...].astype(BF16) for jj in range(ppc)], axis=1)
    z = _dot(qbd_ref[...], ktc) + jnp.concatenate([bias] * t_new, axis=0)
    m_prev = m_ref[...]
    m_new = jnp.maximum(m_prev, jnp.max(z, axis=-1, keepdims=True))
    pexp = jnp.exp(z - m_new)
    alpha = jnp.exp(m_prev - m_new)
    l_ref[...] = alpha * l_ref[...] + jnp.sum(pexp, axis=-1, keepdims=True)
    vtc = jnp.concatenate([vt_refs[jj][...].astype(BF16) for jj in range(ppc)], axis=1)
    upd = lax.dot_general(pexp.astype(BF16), vtc, (((1,), (1,)), ((), ())), preferred_element_type=F32)
    acc_ref[...] = alpha * acc_ref[...] + upd
    m_ref[...] = m_new

    @pl.when(c == n_chunks - 1)
    def _():
        a = acc_ref[...] / l_ref[...]
        outs = []
        for t in range(t_new):
            blk = jnp.where(head_mask, a[t * N_HEADS:(t + 1) * N_HEADS, :], 0.0)
            outs.append(jnp.sum(blk, axis=0, keepdims=True))
        o_ref[...] = jnp.concatenate(outs, axis=0).astype(o_ref.dtype)


def _decode(page_table, q_s, k_new, v_new, lft_pad, tri_new, kt_pages, vt_pages, sfx_pages):
    db, t_new, aw = q_s.shape
    n_pages = page_table.shape[1]
    page = kt_pages.shape[2]
    ppc = min(PAGES_PER_STEP, n_pages)
    n_chunks = n_pages // ppc
    pt_flat = page_table.reshape(-1)

    def page_map(jj):
        return lambda b, c, pt: (pt[b * n_pages + n_pages - 1 - (c * ppc + jj)], 0, 0)

    seq3 = lambda b, c, pt: (b, 0, 0)
    in_specs = [pl.BlockSpec((None, t_new, aw), seq3),
                pl.BlockSpec((None, t_new, aw), seq3),
                pl.BlockSpec((None, t_new, aw), seq3),
                pl.BlockSpec((None, N_HEADS, LANES), seq3),
                pl.BlockSpec(tri_new.shape, lambda b, c, pt: (0, 0))]
    in_specs += [pl.BlockSpec((None, aw, page), page_map(jj)) for jj in range(ppc)]
    in_specs += [pl.BlockSpec((None, aw, page), page_map(jj)) for jj in range(ppc)]
    in_specs += [pl.BlockSpec((None, N_HEADS, 2 * LANES), page_map(jj)) for jj in range(ppc)]
    n_rows = t_new * N_HEADS
    grid_spec = pltpu.PrefetchScalarGridSpec(
        num_scalar_prefetch=1,
        grid=(db, n_chunks),
        in_specs=in_specs,
        out_specs=pl.BlockSpec((None, t_new, aw), seq3),
        scratch_shapes=[pltpu.VMEM((n_rows, aw), BF16), pltpu.VMEM((n_rows, 1), F32), pltpu.VMEM((n_rows, 1), F32),
                        pltpu.VMEM((n_rows, aw), F32), pltpu.VMEM((N_HEADS, LANES), F32)],
    )
    kern = functools.partial(_decode_kernel, n_pages=n_pages, n_chunks=n_chunks, page=page)
    return pl.pallas_call(
        kern,
        grid_spec=grid_spec,
        out_shape=jax.ShapeDtypeStruct((db, t_new, aw), BF16),
        compiler_params=pltpu.CompilerParams(dimension_semantics=("arbitrary", "arbitrary"),
                                             vmem_limit_bytes=VMEM_LIMIT),
        name="decode_paged",
    )(pt_flat, q_s, k_new, v_new, lft_pad, tri_new, *([kt_pages] * ppc), *([vt_pages] * ppc), *([sfx_pages] * ppc))


def _mix_and_route(h, xn, a_bf, d_groups, w_ugg_ga_gb, poolw_ref, pscale_ref, wba_ref, wbp_ref, wout_ref,
                   g2_ref, wrt_ref, br_ref, h1_ref, xn2_ref, comb_ref):
    ga, gb = w_ugg_ga_gb
    mixed = [_dot(d_groups[g].astype(BF16), poolw_ref[g]) for g in range(len(POOL_WINDOWS))]
    pooled = jnp.concatenate(mixed, axis=-1) * pscale_ref[...]
    m = jax.nn.sigmoid(ga) * _dot(a_bf, wba_ref[...]) + jax.nn.sigmoid(gb) * _dot(pooled.astype(BF16), wbp_ref[...])
    h1 = h + _dot(m.astype(BF16), wout_ref[...])
    h1_ref[...] = h1
    xn2 = _rmsnorm(h1, g2_ref[...])
    xn2_bf = xn2.astype(BF16)
    xn2_ref[...] = xn2_bf

    lt = lax.dot_general(wrt_ref[...], xn2_bf, (((1,), (1,)), ((), ())), preferred_element_type=F32) + br_ref[...]
    g_rows = [lt[g:g + 1, :] for g in range(N_EXPERT_GROUPS)]
    gmax = functools.reduce(jnp.maximum, g_rows)
    gsum = functools.reduce(lambda a, b: a + b, [jnp.exp(r - gmax) for r in g_rows])
    pg_sel = 1.0 / gsum
    gsel = jnp.full(gmax.shape, N_EXPERT_GROUPS - 1, jnp.int32)
    for g in range(N_EXPERT_GROUPS - 2, -1, -1):
        gsel = jnp.where(g_rows[g] == gmax, g, gsel)
    e_rows = []
    for k in range(EXPERTS_PER_GROUP):
        r = lt[N_EXPERT_GROUPS + k:N_EXPERT_GROUPS + k + 1, :]
        for g in range(1, N_EXPERT_GROUPS):
            base = N_EXPERT_GROUPS + g * EXPERTS_PER_GROUP + k
            r = jnp.where(gsel == g, lt[base:base + 1, :], r)
        e_rows.append(r)
    v1 = functools.reduce(jnp.maximum, e_rows)
    i1 = jnp.full(gsel.shape, EXPERTS_PER_GROUP - 1, jnp.int32)
    for k in range(EXPERTS_PER_GROUP - 2, -1, -1):
        i1 = jnp.where(e_rows[k] == v1, k, i1)
    rest = [jnp.where(i1 == k, -jnp.inf, e_rows[k]) for k in range(EXPERTS_PER_GROUP)]
    v2 = functools.reduce(jnp.maximum, rest)
    i2 = jnp.full(gsel.shape, EXPERTS_PER_GROUP - 1, jnp.int32)
    for k in range(EXPERTS_PER_GROUP - 2, -1, -1):
        i2 = jnp.where((rest[k] == v2) & (i1 != k), k, i2)
    e2 = jnp.exp(v2 - v1)
    w1 = (1.0 / (1.0 + e2)) * pg_sel
    w2 = (e2 / (1.0 + e2)) * pg_sel
    id1 = gsel * EXPERTS_PER_GROUP + i1
    id2 = gsel * EXPERTS_PER_GROUP + i2
    ntok = lt.shape[1]
    erow = lax.broadcasted_iota(jnp.int32, (LANES, ntok), 0)
    comb_t = jnp.where(erow == id1, w1, 0.0) + jnp.where(erow == id2, w2, 0.0)
    comb_ref[...] = comb_t.T


def _merge_prompt_kernel(h_ref, a_ref, g1_ref, wugg_ref, poolw_ref, pscale_ref, wba_ref, wbp_ref, wout_ref,
                         g2_ref, wrt_ref, br_ref, h1_ref, xn2_ref, comb_ref, utail_ref, ext_ref, tail_ref):
    i = pl.program_id(0)
    pw = LANES * len(POOL_WINDOWS)
    hist_rows = 16

    @pl.when(i == 0)
    def _():
        tail_ref[...] = jnp.zeros(tail_ref.shape, F32)

    h = h_ref[...]
    xn = _rmsnorm(h, g1_ref[...]).astype(BF16)
    ugg = _dot(xn, wugg_ref[...])
    d = h.shape[1]
    u = ugg[:, 0:pw]
    ga = ugg[:, pw:pw + d]
    gb = ugg[:, pw + d:pw + 2 * d]
    utail_ref[...] = u[0:hist_rows, :]

    ext_ref[0:hist_rows, :] = tail_ref[...]
    ext_ref[hist_rows:hist_rows + TM, :] = u
    tail_ref[...] = u[TM - hist_rows:TM, :]

    pos = lax.broadcasted_iota(jnp.int32, (TM, LANES), 0) + i * TM
    d_groups = []
    for g, w in enumerate(POOL_WINDOWS):
        lanes = slice(g * LANES, (g + 1) * LANES)
        tok = ext_ref[hist_rows:hist_rows + TM, lanes]
        acc = tok
        for back in range(1, w):
            acc = acc + ext_ref[hist_rows - back:hist_rows - back + TM, lanes]
        cnt = jnp.minimum(pos + 1, w).astype(F32)
        d_groups.append(acc / cnt - tok)

    _mix_and_route(h, xn, a_ref[...], d_groups, (ga, gb), poolw_ref, pscale_ref, wba_ref, wbp_ref, wout_ref,
                   g2_ref, wrt_ref, br_ref, h1_ref, xn2_ref, comb_ref)


def _merge_sample_kernel(h_ref, a_ref, hist_ref, g1_ref, wugg_ref, poolw_ref, pscale_ref, wba_ref, wbp_ref, wout_ref,
                         g2_ref, wrt_ref, br_ref, h1_ref, xn2_ref, comb_ref, u_ref, *, pos0, db):
    pw = LANES * len(POOL_WINDOWS)
    h = h_ref[...]
    xn = _rmsnorm(h, g1_ref[...]).astype(BF16)
    ugg = _dot(xn, wugg_ref[...])
    d = h.shape[1]
    u = ugg[:, 0:pw]
    ga = ugg[:, pw:pw + d]
    gb = ugg[:, pw + d:pw + 2 * d]
    u_ref[...] = u
    t_new = h.shape[0] // db
    n_hist = hist_ref.shape[0]

    def ext(e, lanes):
        if e < n_hist:
            return hist_ref[e][:, lanes]
        return u[(e - n_hist) * db:(e - n_hist + 1) * db, lanes]

    d_groups = []
    for g, w in enumerate(POOL_WINDOWS):
        lanes = slice(g * LANES, (g + 1) * LANES)
        per_t = []
        for t in range(t_new):
            tok = ext(n_hist + t, lanes)
            acc = tok
            for back in range(1, w):
                acc = acc + ext(n_hist + t - back, lanes)
            cnt = float(min(pos0 + t + 1, w))
            per_t.append(acc / cnt - tok)
        d_groups.append(jnp.concatenate(per_t, axis=0))

    _mix_and_route(h, xn, a_ref[...], d_groups, (ga, gb), poolw_ref, pscale_ref, wba_ref, wbp_ref, wout_ref,
                   g2_ref, wrt_ref, br_ref, h1_ref, xn2_ref, comb_ref)


def _merge_weights_specs(ws):
    return [_full(w.shape) for w in ws]


def _merge_prompt(hp, a_bf, ws):
    tpad, d = hp.shape
    nt = tpad // TM
    aw = a_bf.shape[1]
    pw = LANES * len(POOL_WINDOWS)
    row = lambda i: (i, 0)
    return pl.pallas_call(
        _merge_prompt_kernel,
        grid=(nt,),
        in_specs=[pl.BlockSpec((TM, d), row), pl.BlockSpec((TM, aw), row)] + _merge_weights_specs(ws),
        out_specs=[pl.BlockSpec((TM, d), row), pl.BlockSpec((TM, d), row), pl.BlockSpec((TM, LANES), row),
                   pl.BlockSpec((16, pw), lambda i: (0, 0))],
        out_shape=[jax.ShapeDtypeStruct((tpad, d), F32), jax.ShapeDtypeStruct((tpad, d), BF16),
                   jax.ShapeDtypeStruct((tpad, LANES), F32), jax.ShapeDtypeStruct((16, pw), F32)],
        scratch_shapes=[pltpu.VMEM((TM + 16, pw), F32), pltpu.VMEM((16, pw), F32)],
        compiler_params=pltpu.CompilerParams(dimension_semantics=("arbitrary",), vmem_limit_bytes=VMEM_LIMIT),
        name="merge_prompt",
    )(hp, a_bf, *ws)


def _merge_sample(xs, a_bf, hist_t, ws, pos0, db):
    n, d = xs.shape
    assert n == TM
    aw = a_bf.shape[1]
    pw = LANES * len(POOL_WINDOWS)
    row = lambda i: (i, 0)
    kern = functools.partial(_merge_sample_kernel, pos0=pos0, db=db)
    return pl.pallas_call(
        kern,
        grid=(1,),
        in_specs=[pl.BlockSpec((TM, d), row), pl.BlockSpec((TM, aw), row), _full(hist_t.shape)]
        + _merge_weights_specs(ws),
        out_specs=[pl.BlockSpec((TM, d), row), pl.BlockSpec((TM, d), row), pl.BlockSpec((TM, LANES), row),
                   pl.BlockSpec((TM, pw), row)],
        out_shape=[jax.ShapeDtypeStruct((n, d), F32), jax.ShapeDtypeStruct((n, d), BF16),
                   jax.ShapeDtypeStruct((n, LANES), F32), jax.ShapeDtypeStruct((n, pw), F32)],
        compiler_params=pltpu.CompilerParams(dimension_semantics=("arbitrary",), vmem_limit_bytes=VMEM_LIMIT),
        name="merge_sample",
    )(xs, a_bf, hist_t, *ws)


def _moe_kernel(x_ref, comb_ref, h1_ref, wg_ref, wu_ref, wd_ref, gf_ref, o_ref, acc_ref):
    e = pl.program_id(1)

    @pl.when(e == 0)
    def _():
        acc_ref[...] = jnp.zeros(acc_ref.shape, F32)

    x = x_ref[...]
    hdn = jax.nn.silu(_dot(x, wg_ref[...])) * _dot(x, wu_ref[...])
    out = _dot(hdn.astype(BF16), wd_ref[...])
    lane = lax.broadcasted_iota(jnp.int32, comb_ref.shape, 1)
    col = jnp.sum(jnp.where(lane == e, comb_ref[...], 0.0), axis=-1, keepdims=True)
    acc_ref[...] += col * out

    @pl.when(e == pl.num_programs(1) - 1)
    def _():
        o_ref[...] = _rmsnorm(h1_ref[...] + acc_ref[...], gf_ref[...])


def _moe(xn2, comb, h1, wg, wu, wd, gf):
    n, d = xn2.shape
    nt = n // TM
    ne, _, de = wg.shape
    row = lambda i, e: (i, 0)
    return pl.pallas_call(
        _moe_kernel,
        grid=(nt, ne),
        in_specs=[pl.BlockSpec((TM, d), row), pl.BlockSpec((TM, LANES), row), pl.BlockSpec((TM, d), row),
                  pl.BlockSpec((None, d, de), lambda i, e: (e, 0, 0)),
                  pl.BlockSpec((None, d, de), lambda i, e: (e, 0, 0)),
                  pl.BlockSpec((None, de, d), lambda i, e: (e, 0, 0)),
                  pl.BlockSpec(gf.shape, lambda i, e: (0, 0))],
        out_specs=pl.BlockSpec((TM, d), row),
        out_shape=jax.ShapeDtypeStruct((n, d), F32),
        scratch_shapes=[pltpu.VMEM((TM, d), F32)],
        compiler_params=pltpu.CompilerParams(dimension_semantics=("arbitrary", "arbitrary"),
                                             vmem_limit_bytes=VMEM_LIMIT),
        name="moe_dense",
    )(xn2, comb, h1, wg, wu, wd, gf)


def _placement_matrices():
    import numpy as np
    m = np.zeros((3, LANES, N_HEADS * LANES), np.float32)
    for h in range(N_HEADS):
        base = h * LANES + (HEAD_DIM if h % 2 == 0 else 0)
        for j in range(3):
            m[j, h, base + j] = 1.0
    return jnp.asarray(m, BF16)


def kernel(x_prompt, x_sample, cache_k, cache_v, cache_logf, state_pool, page_table, meta_tokens, norm1_g, w_in,
           b_forget, pool_w, pool_scale, w_branch_attn, w_branch_pool, w_out, norm2_g, w_router_group,
           b_router_group, w_router_expert, b_router_expert, w_gate, w_up, w_down, norm_f_g):
    import numpy as np
    depth = w_in.shape[0]
    assert depth == 1
    batch, seq, d = x_prompt.shape
    assert batch == 1
    db, t_new, _ = x_sample.shape
    assert db * t_new == TM
    aw = N_HEADS * HEAD_DIM
    pw = LANES * len(POOL_WINDOWS)
    seq_len = seq + N_META
    tpad = -(-seq_len // TM) * TM
    assert (seq_len - 16) % TM == 0 and tpad > seq_len
    n_pool, page = cache_k.shape[1], cache_k.shape[2]
    n_pages = page_table.shape[1]
    past_len = n_pages * page
    assert page == LANES and t_new <= 8

    wl = w_in[0]
    w_qkvf = jnp.concatenate([wl[:, 0:3 * aw], jnp.pad(wl[:, 3 * aw:3 * aw + N_HEADS], ((0, 0), (0, LANES - N_HEADS)))],
                             axis=1).astype(BF16)
    w_ugg = wl[:, 3 * aw + N_HEADS:].astype(BF16)
    b_f = jnp.pad(b_forget[0], (0, LANES - N_HEADS)).reshape(1, LANES)
    g1 = norm1_g[0].reshape(1, d)
    g2 = norm2_g[0].reshape(1, d)
    gf = norm_f_g.reshape(1, d)
    n_r = N_EXPERT_GROUPS + N_EXPERTS
    w_rt = jnp.pad(jnp.concatenate([w_router_group[0], w_router_expert[0]], axis=1).T, ((0, 32 - n_r), (0, 0))).astype(BF16)
    b_r = jnp.pad(jnp.concatenate([b_router_group[0], b_router_expert[0]]), (0, 32 - n_r))
    b_r = jnp.broadcast_to(b_r[:, None], (32, TM))
    merge_ws = [g1, w_ugg, pool_w[0].astype(BF16), pool_scale[0].reshape(1, pw), w_branch_attn[0].astype(BF16),
                w_branch_pool[0].astype(BF16), w_out[0].astype(BF16), g2, w_rt, b_r]
    wg = w_gate[0].astype(BF16)
    wu = w_up[0].astype(BF16)
    wd = w_down[0].astype(BF16)

    tri_tm = jnp.asarray(np.tril(np.ones((TM, TM), np.float32)), BF16)
    place = _placement_matrices()

    hp = jnp.concatenate([meta_tokens.astype(x_prompt.dtype), x_prompt[0],
                          jnp.zeros((tpad - seq_len, d), x_prompt.dtype)], axis=0)
    k_p, v_p, lf_p, qt, kaug, vt, tot = _inproj_prompt(hp, g1, w_qkvf, b_f, tri_tm, place, seq_len)
    a_p = _flash(qt, kaug, vt, tot[:, 0, 0:N_HEADS].reshape(-1))
    h1_p, xn2_p, comb_p, utail = _merge_prompt(hp, a_p, merge_ws)
    y_p = _moe(xn2_p, comb_p, h1_p, wg, wu, wd, gf)

    xs = jnp.transpose(x_sample, (1, 0, 2)).reshape(TM, d)
    q_s, k_s, v_s, lf_s, lft_s = _inproj_sample(xs, g1, w_qkvf, b_f)
    to_seq = lambda z: jnp.transpose(z.reshape(t_new, db, -1), (1, 0, 2))
    k_seq, v_seq = to_seq(k_s), to_seq(v_s)
    lft_pad = jnp.pad(jnp.transpose(lft_s.reshape(N_HEADS, t_new, db), (2, 0, 1)), ((0, 0), (0, 0), (0, LANES - t_new)))
    tri_new = jnp.asarray(np.triu(np.ones((LANES, LANES), np.float32)) * (np.arange(LANES) < t_new)[None, :], BF16)

    lf_rows = jnp.transpose(cache_logf[0], (0, 2, 1)).reshape(n_pool * N_HEADS, page)
    sfx_mat = jnp.asarray(np.concatenate([np.tril(np.ones((page, page), np.float32), -1),
                                          np.ones((page, page), np.float32)], axis=1), BF16)
    rows_pad = -(-lf_rows.shape[0] // SUFFIX_ROWS) * SUFFIX_ROWS
    sfx = _suffix(jnp.pad(lf_rows, ((0, rows_pad - lf_rows.shape[0]), (0, 0))), sfx_mat)
    sfx_pages = sfx[:n_pool * N_HEADS].reshape(n_pool, N_HEADS, 2 * LANES)

    kt_pages = jnp.transpose(cache_k[0], (0, 2, 3, 1)).reshape(n_pool, aw, page)
    vt_pages = jnp.transpose(cache_v[0], (0, 2, 3, 1)).reshape(n_pool, aw, page)
    a_s = _decode(page_table, to_seq(q_s), k_seq, v_seq, lft_pad, tri_new, kt_pages, vt_pages, sfx_pages)
    a_s_t = jnp.transpose(a_s, (1, 0, 2)).reshape(TM, aw)
    hist_t = jnp.transpose(state_pool[0], (1, 0, 2))
    h1_s, xn2_s, comb_s, u_s = _merge_sample(xs, a_s_t, hist_t, merge_ws, past_len, db)
    y_s = _moe(xn2_s, comb_s, h1_s, wg, wu, wd, gf)

    y_prompt = y_p[N_META:seq_len].reshape(1, seq, d)
    y_sample = to_seq(y_s)
    k_prompt = k_p.reshape(1, 1, seq_len, N_HEADS, HEAD_DIM)
    v_prompt = v_p.reshape(1, 1, seq_len, N_HEADS, HEAD_DIM)
    logf_prompt = lf_p.reshape(1, 1, seq_len, N_HEADS)
    pool_prompt = utail[16 - POOL_HIST:16].reshape(1, 1, POOL_HIST, pw)
    k_sample = k_seq.reshape(1, db, t_new, N_HEADS, HEAD_DIM)
    v_sample = v_seq.reshape(1, db, t_new, N_HEADS, HEAD_DIM)
    logf_sample = to_seq(lf_s).reshape(1, db, t_new, N_HEADS)
    pool_sample = jnp.concatenate([state_pool[0].astype(F32), to_seq(u_s)], axis=1)[:, -POOL_HIST:].reshape(
        1, db, POOL_HIST, pw)
    return (y_prompt, y_sample, k_prompt, v_prompt, logf_prompt, pool_prompt,
            k_sample, v_sample, logf_sample, pool_sample)
```

```python
import functools

import jax
import jax.numpy as jnp
from jax import lax
from jax.experimental import pallas as pl
from jax.experimental.pallas import tpu as pltpu

F32 = jnp.float32
BF16 = jnp.bfloat16

N_HEADS = 8
HEAD_DIM = 64
N_META = 16
POOL_WINDOWS = (2, 4, 8, 16)
POOL_HIST = 15
N_EXPERT_GROUPS = 4
EXPERTS_PER_GROUP = 4
N_EXPERTS = 16
RMS_EPS = 1e-6

LANES = 128
TM = 512
BQ = 512
BK = 512
PAGES_PER_STEP = 16
SCORE_SLOTS = 3
VROWS = HEAD_DIM + 16
NEG_BIG = -1e30
LOG2E = 1.4426950408889634
VMEM_LIMIT = 52 * 1024 * 1024


def _rmsnorm(x, g):
    inv = lax.rsqrt(jnp.mean(x * x, axis=-1, keepdims=True) + RMS_EPS)
    return (x * inv) * g


def _log_sigmoid(x):
    return jnp.minimum(x, 0.0) - jnp.log1p(jnp.exp(-jnp.abs(x)))


def _split3(x):
    hi = x.astype(BF16)
    r1 = x - hi.astype(F32)
    mid = r1.astype(BF16)
    lo = (r1 - mid.astype(F32)).astype(BF16)
    return hi, mid, lo


def _dot(a, b):
    return jnp.dot(a, b, preferred_element_type=F32)


def _dot3(parts, m):
    return _dot(parts[0], m) + _dot(parts[1], m) + _dot(parts[2], m)


def _dot3_left(m, parts):
    return _dot(m, parts[0]) + _dot(m, parts[1]) + _dot(m, parts[2])


def _inproj_common(x_ref, g_ref, w_ref, bf_ref):
    aw = N_HEADS * HEAD_DIM
    xn = _rmsnorm(x_ref[...], g_ref[...]).astype(BF16)
    proj = _dot(xn, w_ref[...])
    q = proj[:, 0:aw]
    k = proj[:, aw:2 * aw]
    v = proj[:, 2 * aw:3 * aw]
    lf = _log_sigmoid(proj[:, 3 * aw:3 * aw + LANES] + bf_ref[...])
    return q, k, v, lf


def _inproj_prompt_kernel(x_ref, g_ref, w_ref, bf_ref, tri_ref, place_ref,
                          k_ref, v_ref, lf_ref, qt_ref, kaug_ref, vt_ref, tot_ref):
    q, k, v, lf = _inproj_common(x_ref, g_ref, w_ref, bf_ref)
    k_ref[...] = k
    v_ref[...] = v
    lf_ref[...] = lf[:, 0:N_HEADS]

    cs = _dot3_left(tri_ref[...], _split3(lf)) * LOG2E
    tot_ref[0] = cs[TM - 1:TM, :]

    c3 = _split3(cs)
    aug = _dot(c3[0], place_ref[0]) + _dot(c3[1], place_ref[1]) + _dot(c3[2], place_ref[2])

    lane = lax.broadcasted_iota(jnp.int32, (TM, LANES), 1)
    low = lane < HEAD_DIM
    qt = (q * (HEAD_DIM ** -0.5 * LOG2E)).T
    vt = v.T
    row = lax.broadcasted_iota(jnp.int32, (HEAD_DIM, TM), 0)
    minus_ones = jnp.where(row < 3, -1.0, 0.0).astype(BF16)
    ones_row = jnp.where(lax.broadcasted_iota(jnp.int32, (VROWS - HEAD_DIM, TM), 0) == 0, 1.0, 0.0).astype(BF16)
    for h in range(N_HEADS):
        kp = k[:, (h // 2) * LANES:(h // 2 + 1) * LANES]
        own = low if h % 2 == 0 else jnp.logical_not(low)
        kaug_ref[h] = (jnp.where(own, kp, 0.0) + aug[:, h * LANES:(h + 1) * LANES]).astype(BF16)
        qh = qt[h * HEAD_DIM:(h + 1) * HEAD_DIM, :].astype(BF16)
        if h % 2 == 0:
            qt_ref[h, 0:HEAD_DIM, :] = qh
            qt_ref[h, HEAD_DIM:2 * HEAD_DIM, :] = minus_ones
        else:
            qt_ref[h, 0:HEAD_DIM, :] = minus_ones
            qt_ref[h, HEAD_DIM:2 * HEAD_DIM, :] = qh
        vt_ref[h, 0:HEAD_DIM, :] = vt[h * HEAD_DIM:(h + 1) * HEAD_DIM, :].astype(BF16)
        vt_ref[h, HEAD_DIM:VROWS, :] = ones_row


def _inproj_sample_kernel(x_ref, g_ref, w_ref, bf_ref, q_ref, k_ref, v_ref, lf_ref, lft_ref):
    q, k, v, lf = _inproj_common(x_ref, g_ref, w_ref, bf_ref)
    q_ref[...] = q
    k_ref[...] = k
    v_ref[...] = v
    lf_ref[...] = lf[:, 0:N_HEADS]
    lft_ref[...] = lf.T[0:N_HEADS, :]


def _full(shape):
    n = len(shape)
    return pl.BlockSpec(shape, lambda *_: (0,) * n)


def _inproj_prompt(hp, g1, w_qkvf, b_f, tri, place, seq_len):
    tpad = hp.shape[0]
    nt = tpad // TM
    d = hp.shape[1]
    aw = N_HEADS * HEAD_DIM
    row = lambda i: (i, 0)
    return pl.pallas_call(
        _inproj_prompt_kernel,
        grid=(nt,),
        in_specs=[pl.BlockSpec((TM, d), row), _full(g1.shape), _full(w_qkvf.shape), _full(b_f.shape),
                  _full(tri.shape), _full(place.shape)],
        out_specs=[pl.BlockSpec((TM, aw), row), pl.BlockSpec((TM, aw), row), pl.BlockSpec((TM, N_HEADS), row),
                   pl.BlockSpec((N_HEADS, LANES, TM), lambda i: (0, 0, i)),
                   pl.BlockSpec((N_HEADS, TM, LANES), lambda i: (0, i, 0)),
                   pl.BlockSpec((N_HEADS, VROWS, TM), lambda i: (0, 0, i)),
                   pl.BlockSpec((1, 1, LANES), lambda i: (i, 0, 0))],
        out_shape=[jax.ShapeDtypeStruct((seq_len, aw), F32), jax.ShapeDtypeStruct((seq_len, aw), F32),
                   jax.ShapeDtypeStruct((seq_len, N_HEADS), F32),
                   jax.ShapeDtypeStruct((N_HEADS, LANES, tpad), BF16),
                   jax.ShapeDtypeStruct((N_HEADS, tpad, LANES), BF16),
                   jax.ShapeDtypeStruct((N_HEADS, VROWS, tpad), BF16),
                   jax.ShapeDtypeStruct((nt, 1, LANES), F32)],
        compiler_params=pltpu.CompilerParams(dimension_semantics=("arbitrary",), vmem_limit_bytes=VMEM_LIMIT),
        name="inproj_prompt",
    )(hp, g1, w_qkvf, b_f, tri, place)


def _inproj_sample(xs, g1, w_qkvf, b_f):
    n, d = xs.shape
    nt = n // TM
    aw = N_HEADS * HEAD_DIM
    row = lambda i: (i, 0)
    return pl.pallas_call(
        _inproj_sample_kernel,
        grid=(nt,),
        in_specs=[pl.BlockSpec((TM, d), row), _full(g1.shape), _full(w_qkvf.shape), _full(b_f.shape)],
        out_specs=[pl.BlockSpec((TM, aw), row)] * 3 + [pl.BlockSpec((TM, N_HEADS), row),
                                                        pl.BlockSpec((N_HEADS, TM), lambda i: (0, i))],
        out_shape=[jax.ShapeDtypeStruct((n, aw), F32)] * 3 + [jax.ShapeDtypeStruct((n, N_HEADS), F32),
                                                               jax.ShapeDtypeStruct((N_HEADS, n), F32)],
        compiler_params=pltpu.CompilerParams(dimension_semantics=("arbitrary",), vmem_limit_bytes=VMEM_LIMIT),
        name="inproj_sample",
    )(xs, g1, w_qkvf, b_f)


def _flash_kernel(qi_ref, kj_ref, tot_ref, qt_ref, kaug_ref, vt_ref, o_ref, m_ref, acc_ref, z_ref):
    p = pl.program_id(0)
    i = qi_ref[p]
    j = kj_ref[p]

    @pl.when(j == 0)
    def _():
        m_ref[...] = jnp.full(m_ref.shape, NEG_BIG, F32)
        acc_ref[...] = jnp.zeros(acc_ref.shape, F32)

    def scores(h):
        z_ref[h % SCORE_SLOTS] = _dot(kaug_ref[h], qt_ref[h])

    def softmax_pv(h, diagonal):
        z = z_ref[h % SCORE_SLOTS]
        if diagonal:
            s_pos = lax.broadcasted_iota(jnp.int32, (BK, BQ), 0)
            t_pos = lax.broadcasted_iota(jnp.int32, (BK, BQ), 1)
            z = jnp.where(s_pos <= t_pos, z, NEG_BIG)
        m = m_ref[h]
        m_new = jnp.maximum(m, jnp.max(z, axis=0, keepdims=True))
        pexp = jnp.exp2(z - m_new)
        alpha = jnp.exp2(m - m_new)
        acc_ref[h] = alpha * acc_ref[h] + _dot(vt_ref[h], pexp.astype(BF16))
        m_ref[h] = m_new + tot_ref[j * N_HEADS + h]

    def tile(diagonal):
        for h in range(SCORE_SLOTS - 1):
            scores(h)
        for h in range(N_HEADS):
            if h + SCORE_SLOTS - 1 < N_HEADS:
                scores(h + SCORE_SLOTS - 1)
            softmax_pv(h, diagonal)

    @pl.when(j < i)
    def _():
        tile(False)

    @pl.when(j == i)
    def _():
        tile(True)
        outs = []
        for h in range(N_HEADS):
            a = acc_ref[h]
            outs.append(a[0:HEAD_DIM, :] / a[HEAD_DIM:HEAD_DIM + 1, :])
        o_ref[...] = jnp.concatenate(outs, axis=0).T.astype(BF16)


SUFFIX_ROWS = 2048


def _suffix_kernel(x_ref, m_ref, o_ref):
    o_ref[...] = _dot3(_split3(x_ref[...]), m_ref[...])


def _suffix(lf_rows, mat):
    n = lf_rows.shape[0]
    return pl.pallas_call(
        _suffix_kernel,
        grid=(n // SUFFIX_ROWS,),
        in_specs=[pl.BlockSpec((SUFFIX_ROWS, LANES), lambda i: (i, 0)), _full(mat.shape)],
        out_specs=pl.BlockSpec((SUFFIX_ROWS, 2 * LANES), lambda i: (i, 0)),
        out_shape=jax.ShapeDtypeStruct((n, 2 * LANES), F32),
        compiler_params=pltpu.CompilerParams(dimension_semantics=("arbitrary",), vmem_limit_bytes=VMEM_LIMIT),
        name="suffix_pages",
    )(lf_rows, mat)


def _page_copies(step, slot, pt_ref, kt_hbm, vt_hbm, sfx_hbm, kbuf, vbuf, sbuf, sems, n_pages, n_chunks):
    ppc = n_pages // n_chunks
    b = step // n_chunks
    c = step % n_chunks
    copies = []
    for jj in range(ppc):
        pg = pt_ref[b * n_pages + n_pages - 1 - (c * ppc + jj)]
        copies.append(pltpu.make_async_copy(kt_hbm.at[pg], kbuf.at[slot, jj], sems.at[0, slot]))
        copies.append(pltpu.make_async_copy(vt_hbm.at[pg], vbuf.at[slot, jj], sems.at[1, slot]))
        copies.append(pltpu.make_async_copy(sfx_hbm.at[pg], sbuf.at[slot, jj], sems.at[2, slot]))
    return copies


def _decode_step(c, slot, q_ref, kn_ref, vn_ref, lft_ref, tri_ref, o_ref, qbd_ref, m_ref, l_ref, acc_ref, carry_ref,
                 kbuf, vbuf, sbuf, *, n_pages, n_chunks):
    ppc = n_pages // n_chunks
    page = kbuf.shape[3]
    t_new = q_ref.shape[0]
    aw = N_HEADS * HEAD_DIM
    n_rows = t_new * N_HEADS

    lane_head = lax.broadcasted_iota(jnp.int32, (N_HEADS, aw), 1) // HEAD_DIM
    head_mask = lane_head == lax.broadcasted_iota(jnp.int32, (N_HEADS, aw), 0)

    @pl.when(c == 0)
    def _():
        q = q_ref[...] * (HEAD_DIM ** -0.5)
        qbd = jnp.concatenate(
            [jnp.where(head_mask, jnp.broadcast_to(q[t:t + 1, :], (N_HEADS, aw)), 0.0) for t in range(t_new)], axis=0)
        qbd_ref[...] = qbd.astype(BF16)
        carry_ref[...] = jnp.zeros(carry_ref.shape, F32)

        cnew = _dot3(_split3(lft_ref[...]), tri_ref[...])
        cnew = jnp.concatenate([cnew] * t_new, axis=0)
        t_of_row = lax.broadcasted_iota(jnp.int32, (n_rows, 1), 0) // N_HEADS
        kn = kn_ref[...]
        vn = vn_ref[...]
        zs = []
        for t2 in range(t_new):
            zc = jnp.sum(qbd * kn[t2:t2 + 1, :], axis=-1, keepdims=True)
            zc = zc - cnew[:, t2:t2 + 1]
            zs.append(jnp.where(t_of_row >= t2, zc, NEG_BIG))
        m = functools.reduce(jnp.maximum, zs)
        l = jnp.zeros((n_rows, 1), F32)
        acc = jnp.zeros((n_rows, aw), F32)
        for t2 in range(t_new):
            pe = jnp.exp(zs[t2] - m)
            l = l + pe
            acc = acc + pe * vn[t2:t2 + 1, :]
        m_ref[...] = m
        l_ref[...] = l
        acc_ref[...] = acc

    carry = carry_ref[...]
    biases = []
    for jj in range(ppc):
        sfx = sbuf[slot, jj]
        biases.append(sfx[:, 0:page] + carry)
        carry = carry + sfx[:, LANES:LANES + page]
    carry_ref[...] = carry
    bias = jnp.concatenate(biases, axis=1)
    ktc = jnp.concatenate([kbuf[slot, jj].astype(BF16) for jj in range(ppc)], axis=1)
    z = _dot(qbd_ref[...], ktc) + jnp.concatenate([bias] * t_new, axis=0)
    m_prev = m_ref[...]
    m_new = jnp.maximum(m_prev, jnp.max(z, axis=-1, keepdims=True))
    pexp = jnp.exp(z - m_new)
    alpha = jnp.exp(m_prev - m_new)
    l_ref[...] = alpha * l_ref[...] + jnp.sum(pexp, axis=-1, keepdims=True)
    vtc = jnp.concatenate([vbuf[slot, jj].astype(BF16) for jj in range(ppc)], axis=1)
    upd = lax.dot_general(pexp.astype(BF16), vtc, (((1,), (1,)), ((), ())), preferred_element_type=F32)
    acc_ref[...] = alpha * acc_ref[...] + upd
    m_ref[...] = m_new

    @pl.when(c == n_chunks - 1)
    def _():
        a = acc_ref[...] / l_ref[...]
        outs = []
        for t in range(t_new):
            blk = jnp.where(head_mask, a[t * N_HEADS:(t + 1) * N_HEADS, :], 0.0)
            outs.append(jnp.sum(blk, axis=0, keepdims=True))
        o_ref[...] = jnp.concatenate(outs, axis=0).astype(o_ref.dtype)


def _attention_kernel(qi_ref, kj_ref, tot_ref, pt_ref,
                      qt_ref, kaug_ref, vt_ref, q_ref, kn_ref, vn_ref, lft_ref, tri_ref, kt_hbm, vt_hbm, sfx_hbm,
                      o_ref, os_ref,
                      fm_ref, facc_ref, z_ref,
                      qbd_ref, dm_ref, dl_ref, dacc_ref, carry_ref, kbuf, vbuf, sbuf, sems,
                      *, n_pages, n_chunks, n_dec, n_pairs):
    p = pl.program_id(0)

    def copies(step):
        return _page_copies(step, step % 2, pt_ref, kt_hbm, vt_hbm, sfx_hbm, kbuf, vbuf, sbuf, sems,
                            n_pages, n_chunks)

    @pl.when(p == 0)
    def _():
        for cp in copies(p):
            cp.start()

    @pl.when(p + 1 < n_dec)
    def _():
        for cp in copies(p + 1):
            cp.start()

    @pl.when(p < n_pairs)
    def _():
        _flash_kernel(qi_ref, kj_ref, tot_ref, qt_ref, kaug_ref, vt_ref, o_ref, fm_ref, facc_ref, z_ref)

    @pl.when(p < n_dec)
    def _():
        for cp in copies(p):
            cp.wait()
        _decode_step(p % n_chunks, p % 2, q_ref, kn_ref, vn_ref, lft_ref, tri_ref, os_ref, qbd_ref, dm_ref, dl_ref,
                     dacc_ref, carry_ref, kbuf, vbuf, sbuf, n_pages=n_pages, n_chunks=n_chunks)


def _attention(qt, kaug, vt, tot_flat, page_table, q_s, k_new, v_new, lft_pad, tri_new, kt_pages, vt_pages, sfx_pages):
    tpad = qt.shape[2]
    nq = tpad // BQ
    pairs = [(i, j) for i in range(nq) for j in range(i + 1)]
    aw = N_HEADS * HEAD_DIM
    db, t_new, _ = q_s.shape
    n_pages = page_table.shape[1]
    page = kt_pages.shape[2]
    ppc = min(PAGES_PER_STEP, n_pages)
    n_chunks = n_pages // ppc
    n_dec = db * n_chunks
    n_pairs = len(pairs)
    n_steps = max(n_pairs, n_dec)
    pairs = pairs + [pairs[-1]] * (n_steps - n_pairs)
    qi = jnp.asarray([p[0] for p in pairs], jnp.int32)
    kj = jnp.asarray([p[1] for p in pairs], jnp.int32)
    n_rows = t_new * N_HEADS

    seq3 = lambda p, qi, kj, tot, pt: (jnp.minimum(p // n_chunks, db - 1), 0, 0)
    grid_spec = pltpu.PrefetchScalarGridSpec(
        num_scalar_prefetch=4,
        grid=(n_steps,),
        in_specs=[pl.BlockSpec((N_HEADS, LANES, BQ), lambda p, qi, kj, tot, pt: (0, 0, qi[p])),
                  pl.BlockSpec((N_HEADS, BK, LANES), lambda p, qi, kj, tot, pt: (0, kj[p], 0)),
                  pl.BlockSpec((N_HEADS, VROWS, BK), lambda p, qi, kj, tot, pt: (0, 0, kj[p])),
                  pl.BlockSpec((None, t_new, aw), seq3),
                  pl.BlockSpec((None, t_new, aw), seq3),
                  pl.BlockSpec((None, t_new, aw), seq3),
                  pl.BlockSpec((None, N_HEADS, LANES), seq3),
                  pl.BlockSpec(tri_new.shape, lambda p, qi, kj, tot, pt: (0, 0)),
                  pl.BlockSpec(memory_space=pl.ANY),
                  pl.BlockSpec(memory_space=pl.ANY),
                  pl.BlockSpec(memory_space=pl.ANY)],
        out_specs=[pl.BlockSpec((BQ, aw), lambda p, qi, kj, tot, pt: (qi[p], 0)),
                   pl.BlockSpec((None, t_new, aw), seq3)],
        scratch_shapes=[pltpu.VMEM((N_HEADS, 1, BQ), F32), pltpu.VMEM((N_HEADS, VROWS, BQ), F32),
                        pltpu.VMEM((SCORE_SLOTS, BK, BQ), F32),
                        pltpu.VMEM((n_rows, aw), BF16), pltpu.VMEM((n_rows, 1), F32), pltpu.VMEM((n_rows, 1), F32),
                        pltpu.VMEM((n_rows, aw), F32), pltpu.VMEM((N_HEADS, LANES), F32),
                        pltpu.VMEM((2, ppc, aw, page), F32), pltpu.VMEM((2, ppc, aw, page), F32),
                        pltpu.VMEM((2, ppc, N_HEADS, 2 * LANES), F32), pltpu.SemaphoreType.DMA((3, 2))],
    )
    kern = functools.partial(_attention_kernel, n_pages=n_pages, n_chunks=n_chunks, n_dec=n_dec, n_pairs=n_pairs)
    return pl.pallas_call(
        kern,
        grid_spec=grid_spec,
        out_shape=[jax.ShapeDtypeStruct((tpad, aw), BF16), jax.ShapeDtypeStruct((db, t_new, aw), BF16)],
        compiler_params=pltpu.CompilerParams(dimension_semantics=("arbitrary",), vmem_limit_bytes=VMEM_LIMIT),
        name="attention",
    )(qi, kj, tot_flat, page_table.reshape(-1), qt, kaug, vt, q_s, k_new, v_new, lft_pad, tri_new,
      kt_pages, vt_pages, sfx_pages)


def _mix_and_route(h, xn, a_bf, d_groups, w_ugg_ga_gb, poolw_ref, pscale_ref, wba_ref, wbp_ref, wout_ref,
                   g2_ref, wrt_ref, br_ref, h1_ref, xn2_ref, comb_ref):
    ga, gb = w_ugg_ga_gb
    mixed = [_dot(d_groups[g].astype(BF16), poolw_ref[g]) for g in range(len(POOL_WINDOWS))]
    pooled = jnp.concatenate(mixed, axis=-1) * pscale_ref[...]
    m = jax.nn.sigmoid(ga) * _dot(a_bf, wba_ref[...]) + jax.nn.sigmoid(gb) * _dot(pooled.astype(BF16), wbp_ref[...])
    h1 = h + _dot(m.astype(BF16), wout_ref[...])
    h1_ref[...] = h1
    xn2 = _rmsnorm(h1, g2_ref[...])
    xn2_bf = xn2.astype(BF16)
    xn2_ref[...] = xn2_bf

    lt = lax.dot_general(wrt_ref[...], xn2_bf, (((1,), (1,)), ((), ())), preferred_element_type=F32) + br_ref[...]
    g_rows = [lt[g:g + 1, :] for g in range(N_EXPERT_GROUPS)]
    gmax = functools.reduce(jnp.maximum, g_rows)
    gsum = functools.reduce(lambda a, b: a + b, [jnp.exp(r - gmax) for r in g_rows])
    pg_sel = 1.0 / gsum
    gsel = jnp.full(gmax.shape, N_EXPERT_GROUPS - 1, jnp.int32)
    for g in range(N_EXPERT_GROUPS - 2, -1, -1):
        gsel = jnp.where(g_rows[g] == gmax, g, gsel)
    e_rows = []
    for k in range(EXPERTS_PER_GROUP):
        r = lt[N_EXPERT_GROUPS + k:N_EXPERT_GROUPS + k + 1, :]
        for g in range(1, N_EXPERT_GROUPS):
            base = N_EXPERT_GROUPS + g * EXPERTS_PER_GROUP + k
            r = jnp.where(gsel == g, lt[base:base + 1, :], r)
        e_rows.append(r)
    v1 = functools.reduce(jnp.maximum, e_rows)
    i1 = jnp.full(gsel.shape, EXPERTS_PER_GROUP - 1, jnp.int32)
    for k in range(EXPERTS_PER_GROUP - 2, -1, -1):
        i1 = jnp.where(e_rows[k] == v1, k, i1)
    rest = [jnp.where(i1 == k, -jnp.inf, e_rows[k]) for k in range(EXPERTS_PER_GROUP)]
    v2 = functools.reduce(jnp.maximum, rest)
    i2 = jnp.full(gsel.shape, EXPERTS_PER_GROUP - 1, jnp.int32)
    for k in range(EXPERTS_PER_GROUP - 2, -1, -1):
        i2 = jnp.where((rest[k] == v2) & (i1 != k), k, i2)
    e2 = jnp.exp(v2 - v1)
    w1 = (1.0 / (1.0 + e2)) * pg_sel
    w2 = (e2 / (1.0 + e2)) * pg_sel
    id1 = gsel * EXPERTS_PER_GROUP + i1
    id2 = gsel * EXPERTS_PER_GROUP + i2
    ntok = lt.shape[1]
    erow = lax.broadcasted_iota(jnp.int32, (LANES, ntok), 0)
    comb_t = jnp.where(erow == id1, w1, 0.0) + jnp.where(erow == id2, w2, 0.0)
    comb_ref[...] = comb_t.T


def _merge_prompt_kernel(h_ref, a_ref, g1_ref, wugg_ref, poolw_ref, pscale_ref, wba_ref, wbp_ref, wout_ref,
                         g2_ref, wrt_ref, br_ref, h1_ref, xn2_ref, comb_ref, utail_ref, ext_ref, tail_ref):
    i = pl.program_id(0)
    pw = LANES * len(POOL_WINDOWS)
    hist_rows = 16

    @pl.when(i == 0)
    def _():
        tail_ref[...] = jnp.zeros(tail_ref.shape, F32)

    h = h_ref[...]
    xn = _rmsnorm(h, g1_ref[...]).astype(BF16)
    ugg = _dot(xn, wugg_ref[...])
    d = h.shape[1]
    u = ugg[:, 0:pw]
    ga = ugg[:, pw:pw + d]
    gb = ugg[:, pw + d:pw + 2 * d]
    utail_ref[...] = u[0:hist_rows, :]

    ext_ref[0:hist_rows, :] = tail_ref[...]
    ext_ref[hist_rows:hist_rows + TM, :] = u
    tail_ref[...] = u[TM - hist_rows:TM, :]

    pos = lax.broadcasted_iota(jnp.int32, (TM, LANES), 0) + i * TM
    d_groups = []
    for g, w in enumerate(POOL_WINDOWS):
        lanes = slice(g * LANES, (g + 1) * LANES)
        tok = ext_ref[hist_rows:hist_rows + TM, lanes]
        acc = tok
        for back in range(1, w):
            acc = acc + ext_ref[hist_rows - back:hist_rows - back + TM, lanes]
        cnt = jnp.minimum(pos + 1, w).astype(F32)
        d_groups.append(acc / cnt - tok)

    _mix_and_route(h, xn, a_ref[...], d_groups, (ga, gb), poolw_ref, pscale_ref, wba_ref, wbp_ref, wout_ref,
                   g2_ref, wrt_ref, br_ref, h1_ref, xn2_ref, comb_ref)


def _merge_sample_kernel(h_ref, a_ref, hist_ref, g1_ref, wugg_ref, poolw_ref, pscale_ref, wba_ref, wbp_ref, wout_ref,
                         g2_ref, wrt_ref, br_ref, h1_ref, xn2_ref, comb_ref, u_ref, *, pos0, db):
    pw = LANES * len(POOL_WINDOWS)
    h = h_ref[...]
    xn = _rmsnorm(h, g1_ref[...]).astype(BF16)
    ugg = _dot(xn, wugg_ref[...])
    d = h.shape[1]
    u = ugg[:, 0:pw]
    ga = ugg[:, pw:pw + d]
    gb = ugg[:, pw + d:pw + 2 * d]
    u_ref[...] = u
    t_new = h.shape[0] // db
    n_hist = hist_ref.shape[0]

    def ext(e, lanes):
        if e < n_hist:
            return hist_ref[e][:, lanes]
        return u[(e - n_hist) * db:(e - n_hist + 1) * db, lanes]

    d_groups = []
    for g, w in enumerate(POOL_WINDOWS):
        lanes = slice(g * LANES, (g + 1) * LANES)
        per_t = []
        for t in range(t_new):
            tok = ext(n_hist + t, lanes)
            acc = tok
            for back in range(1, w):
                acc = acc + ext(n_hist + t - back, lanes)
            cnt = float(min(pos0 + t + 1, w))
            per_t.append(acc / cnt - tok)
        d_groups.append(jnp.concatenate(per_t, axis=0))

    _mix_and_route(h, xn, a_ref[...], d_groups, (ga, gb), poolw_ref, pscale_ref, wba_ref, wbp_ref, wout_ref,
                   g2_ref, wrt_ref, br_ref, h1_ref, xn2_ref, comb_ref)


def _merge_weights_specs(ws):
    return [_full(w.shape) for w in ws]


def _merge_prompt(hp, a_bf, ws):
    tpad, d = hp.shape
    nt = tpad // TM
    aw = a_bf.shape[1]
    pw = LANES * len(POOL_WINDOWS)
    row = lambda i: (i, 0)
    return pl.pallas_call(
        _merge_prompt_kernel,
        grid=(nt,),
        in_specs=[pl.BlockSpec((TM, d), row), pl.BlockSpec((TM, aw), row)] + _merge_weights_specs(ws),
        out_specs=[pl.BlockSpec((TM, d), row), pl.BlockSpec((TM, d), row), pl.BlockSpec((TM, LANES), row),
                   pl.BlockSpec((16, pw), lambda i: (0, 0))],
        out_shape=[jax.ShapeDtypeStruct((tpad, d), F32), jax.ShapeDtypeStruct((tpad, d), BF16),
                   jax.ShapeDtypeStruct((tpad, LANES), F32), jax.ShapeDtypeStruct((16, pw), F32)],
        scratch_shapes=[pltpu.VMEM((TM + 16, pw), F32), pltpu.VMEM((16, pw), F32)],
        compiler_params=pltpu.CompilerParams(dimension_semantics=("arbitrary",), vmem_limit_bytes=VMEM_LIMIT),
        name="merge_prompt",
    )(hp, a_bf, *ws)


def _merge_sample(xs, a_bf, hist_t, ws, pos0, db):
    n, d = xs.shape
    assert n == TM
    aw = a_bf.shape[1]
    pw = LANES * len(POOL_WINDOWS)
    row = lambda i: (i, 0)
    kern = functools.partial(_merge_sample_kernel, pos0=pos0, db=db)
    return pl.pallas_call(
        kern,
        grid=(1,),
        in_specs=[pl.BlockSpec((TM, d), row), pl.BlockSpec((TM, aw), row), _full(hist_t.shape)]
        + _merge_weights_specs(ws),
        out_specs=[pl.BlockSpec((TM, d), row), pl.BlockSpec((TM, d), row), pl.BlockSpec((TM, LANES), row),
                   pl.BlockSpec((TM, pw), row)],
        out_shape=[jax.ShapeDtypeStruct((n, d), F32), jax.ShapeDtypeStruct((n, d), BF16),
                   jax.ShapeDtypeStruct((n, LANES), F32), jax.ShapeDtypeStruct((n, pw), F32)],
        compiler_params=pltpu.CompilerParams(dimension_semantics=("arbitrary",), vmem_limit_bytes=VMEM_LIMIT),
        name="merge_sample",
    )(xs, a_bf, hist_t, *ws)


def _moe_kernel(x_ref, comb_ref, h1_ref, wg_ref, wu_ref, wd_ref, gf_ref, o_ref, acc_ref):
    e = pl.program_id(1)

    @pl.when(e == 0)
    def _():
        acc_ref[...] = jnp.zeros(acc_ref.shape, F32)

    x = x_ref[...]
    hdn = jax.nn.silu(_dot(x, wg_ref[...])) * _dot(x, wu_ref[...])
    out = _dot(hdn.astype(BF16), wd_ref[...])
    lane = lax.broadcasted_iota(jnp.int32, comb_ref.shape, 1)
    col = jnp.sum(jnp.where(lane == e, comb_ref[...], 0.0), axis=-1, keepdims=True)
    acc_ref[...] += col * out

    @pl.when(e == pl.num_programs(1) - 1)
    def _():
        o_ref[...] = _rmsnorm(h1_ref[...] + acc_ref[...], gf_ref[...])


def _moe(xn2, comb, h1, wg, wu, wd, gf):
    n, d = xn2.shape
    nt = n // TM
    ne, _, de = wg.shape
    row = lambda i, e: (i, 0)
    return pl.pallas_call(
        _moe_kernel,
        grid=(nt, ne),
        in_specs=[pl.BlockSpec((TM, d), row), pl.BlockSpec((TM, LANES), row), pl.BlockSpec((TM, d), row),
                  pl.BlockSpec((None, d, de), lambda i, e: (e, 0, 0)),
                  pl.BlockSpec((None, d, de), lambda i, e: (e, 0, 0)),
                  pl.BlockSpec((None, de, d), lambda i, e: (e, 0, 0)),
                  pl.BlockSpec(gf.shape, lambda i, e: (0, 0))],
        out_specs=pl.BlockSpec((TM, d), row),
        out_shape=jax.ShapeDtypeStruct((n, d), F32),
        scratch_shapes=[pltpu.VMEM((TM, d), F32)],
        compiler_params=pltpu.CompilerParams(dimension_semantics=("arbitrary", "arbitrary"),
                                             vmem_limit_bytes=VMEM_LIMIT),
        name="moe_dense",
    )(xn2, comb, h1, wg, wu, wd, gf)


def _placement_matrices():
    import numpy as np
    m = np.zeros((3, LANES, N_HEADS * LANES), np.float32)
    for h in range(N_HEADS):
        base = h * LANES + (HEAD_DIM if h % 2 == 0 else 0)
        for j in range(3):
            m[j, h, base + j] = 1.0
    return jnp.asarray(m, BF16)


def kernel(x_prompt, x_sample, cache_k, cache_v, cache_logf, state_pool, page_table, meta_tokens, norm1_g, w_in,
           b_forget, pool_w, pool_scale, w_branch_attn, w_branch_pool, w_out, norm2_g, w_router_group,
           b_router_group, w_router_expert, b_router_expert, w_gate, w_up, w_down, norm_f_g):
    import numpy as np
    depth = w_in.shape[0]
    assert depth == 1
    batch, seq, d = x_prompt.shape
    assert batch == 1
    db, t_new, _ = x_sample.shape
    assert db * t_new == TM
    aw = N_HEADS * HEAD_DIM
    pw = LANES * len(POOL_WINDOWS)
    seq_len = seq + N_META
    tpad = -(-seq_len // TM) * TM
    assert (seq_len - 16) % TM == 0 and tpad > seq_len
    n_pool, page = cache_k.shape[1], cache_k.shape[2]
    n_pages = page_table.shape[1]
    past_len = n_pages * page
    assert page == LANES and t_new <= 8

    wl = w_in[0]
    w_qkvf = jnp.concatenate([wl[:, 0:3 * aw], jnp.pad(wl[:, 3 * aw:3 * aw + N_HEADS], ((0, 0), (0, LANES - N_HEADS)))],
                             axis=1).astype(BF16)
    w_ugg = wl[:, 3 * aw + N_HEADS:].astype(BF16)
    b_f = jnp.pad(b_forget[0], (0, LANES - N_HEADS)).reshape(1, LANES)
    g1 = norm1_g[0].reshape(1, d)
    g2 = norm2_g[0].reshape(1, d)
    gf = norm_f_g.reshape(1, d)
    n_r = N_EXPERT_GROUPS + N_EXPERTS
    w_rt = jnp.pad(jnp.concatenate([w_router_group[0], w_router_expert[0]], axis=1).T, ((0, 32 - n_r), (0, 0))).astype(BF16)
    b_r = jnp.pad(jnp.concatenate([b_router_group[0], b_router_expert[0]]), (0, 32 - n_r))
    b_r = jnp.broadcast_to(b_r[:, None], (32, TM))
    merge_ws = [g1, w_ugg, pool_w[0].astype(BF16), pool_scale[0].reshape(1, pw), w_branch_attn[0].astype(BF16),
                w_branch_pool[0].astype(BF16), w_out[0].astype(BF16), g2, w_rt, b_r]
    wg = w_gate[0].astype(BF16)
    wu = w_up[0].astype(BF16)
    wd = w_down[0].astype(BF16)

    tri_tm = jnp.asarray(np.tril(np.ones((TM, TM), np.float32)), BF16)
    place = _placement_matrices()

    hp = jnp.concatenate([meta_tokens.astype(x_prompt.dtype), x_prompt[0],
                          jnp.zeros((tpad - seq_len, d), x_prompt.dtype)], axis=0)
    k_p, v_p, lf_p, qt, kaug, vt, tot = _inproj_prompt(hp, g1, w_qkvf, b_f, tri_tm, place, seq_len)

    xs = jnp.transpose(x_sample, (1, 0, 2)).reshape(TM, d)
    q_s, k_s, v_s, lf_s, lft_s = _inproj_sample(xs, g1, w_qkvf, b_f)
    to_seq = lambda z: jnp.transpose(z.reshape(t_new, db, -1), (1, 0, 2))
    k_seq, v_seq = to_seq(k_s), to_seq(v_s)
    lft_pad = jnp.pad(jnp.transpose(lft_s.reshape(N_HEADS, t_new, db), (2, 0, 1)), ((0, 0), (0, 0), (0, LANES - t_new)))
    tri_new = jnp.asarray(np.triu(np.ones((LANES, LANES), np.float32)) * (np.arange(LANES) < t_new)[None, :], BF16)

    lf_rows = jnp.transpose(cache_logf[0], (0, 2, 1)).reshape(n_pool * N_HEADS, page)
    sfx_mat = jnp.asarray(np.concatenate([np.tril(np.ones((page, page), np.float32), -1),
                                          np.ones((page, page), np.float32)], axis=1), BF16)
    rows_pad = -(-lf_rows.shape[0] // SUFFIX_ROWS) * SUFFIX_ROWS
    sfx = _suffix(jnp.pad(lf_rows, ((0, rows_pad - lf_rows.shape[0]), (0, 0))), sfx_mat)
    sfx_pages = sfx[:n_pool * N_HEADS].reshape(n_pool, N_HEADS, 2 * LANES)

    kt_pages = jnp.transpose(cache_k[0], (0, 2, 3, 1)).reshape(n_pool, aw, page)
    vt_pages = jnp.transpose(cache_v[0], (0, 2, 3, 1)).reshape(n_pool, aw, page)
    a_p, a_s = _attention(qt, kaug, vt, tot[:, 0, 0:N_HEADS].reshape(-1), page_table, to_seq(q_s), k_seq, v_seq,
                          lft_pad, tri_new, kt_pages, vt_pages, sfx_pages)
    h1_p, xn2_p, comb_p, utail = _merge_prompt(hp, a_p, merge_ws)
    y_p = _moe(xn2_p, comb_p, h1_p, wg, wu, wd, gf)
    a_s_t = jnp.transpose(a_s, (1, 0, 2)).reshape(TM, aw)
    hist_t = jnp.transpose(state_pool[0], (1, 0, 2))
    h1_s, xn2_s, comb_s, u_s = _merge_sample(xs, a_s_t, hist_t, merge_ws, past_len, db)
    y_s = _moe(xn2_s, comb_s, h1_s, wg, wu, wd, gf)

    y_prompt = y_p[N_META:seq_len].reshape(1, seq, d)
    y_sample = to_seq(y_s)
    k_prompt = k_p.reshape(1, 1, seq_len, N_HEADS, HEAD_DIM)
    v_prompt = v_p.reshape(1, 1, seq_len, N_HEADS, HEAD_DIM)
    logf_prompt = lf_p.reshape(1, 1, seq_len, N_HEADS)
    pool_prompt = utail[16 - POOL_HIST:16].reshape(1, 1, POOL_HIST, pw)
    k_sample = k_seq.reshape(1, db, t_new, N_HEADS, HEAD_DIM)
    v_sample = v_seq.reshape(1, db, t_new, N_HEADS, HEAD_DIM)
    logf_sample = to_seq(lf_s).reshape(1, db, t_new, N_HEADS)
    pool_sample = jnp.concatenate([state_pool[0].astype(F32), to_seq(u_s)], axis=1)[:, -POOL_HIST:].reshape(
        1, db, POOL_HIST, pw)
    return (y_prompt, y_sample, k_prompt, v_prompt, logf_prompt, pool_prompt,
            k_sample, v_sample, logf_sample, pool_sample)
```

```python
import functools

import jax
import jax.numpy as jnp
from jax import lax
from jax.experimental import pallas as pl
from jax.experimental.pallas import tpu as pltpu

F32 = jnp.float32
BF16 = jnp.bfloat16

N_HEADS = 8
HEAD_DIM = 64
N_META = 16
POOL_WINDOWS = (2, 4, 8, 16)
POOL_HIST = 15
N_EXPERT_GROUPS = 4
EXPERTS_PER_GROUP = 4
N_EXPERTS = 16
RMS_EPS = 1e-6

LANES = 128
TM = 512
BQ = 512
BK = 512
PAGES_PER_STEP = 16
SCORE_SLOTS = 3
VROWS = HEAD_DIM + 16
NEG_BIG = -1e30
LOG2E = 1.4426950408889634
VMEM_LIMIT = 52 * 1024 * 1024


def _rmsnorm(x, g):
    inv = lax.rsqrt(jnp.mean(x * x, axis=-1, keepdims=True) + RMS_EPS)
    return (x * inv) * g


def _log_sigmoid(x):
    return jnp.minimum(x, 0.0) - jnp.log1p(jnp.exp(-jnp.abs(x)))


def _split3(x):
    hi = x.astype(BF16)
    r1 = x - hi.astype(F32)
    mid = r1.astype(BF16)
    lo = (r1 - mid.astype(F32)).astype(BF16)
    return hi, mid, lo


def _dot(a, b):
    return jnp.dot(a, b, preferred_element_type=F32)


def _dot3(parts, m):
    return _dot(parts[0], m) + _dot(parts[1], m) + _dot(parts[2], m)


def _dot3_left(m, parts):
    return _dot(m, parts[0]) + _dot(m, parts[1]) + _dot(m, parts[2])


def _inproj_common(x_ref, g_ref, w_ref, bf_ref):
    aw = N_HEADS * HEAD_DIM
    xn = _rmsnorm(x_ref[...], g_ref[...]).astype(BF16)
    proj = _dot(xn, w_ref[...])
    q = proj[:, 0:aw]
    k = proj[:, aw:2 * aw]
    v = proj[:, 2 * aw:3 * aw]
    lf = _log_sigmoid(proj[:, 3 * aw:3 * aw + LANES] + bf_ref[...])
    return q, k, v, lf


def _inproj_prompt_kernel(x_ref, g_ref, w_ref, bf_ref, tri_ref, place_ref,
                          k_ref, v_ref, lf_ref, qt_ref, kaug_ref, vt_ref, tot_ref):
    q, k, v, lf = _inproj_common(x_ref, g_ref, w_ref, bf_ref)
    k_ref[...] = k
    v_ref[...] = v
    lf_ref[...] = lf[:, 0:N_HEADS]

    cs = _dot3_left(tri_ref[...], _split3(lf)) * LOG2E
    tot_ref[0] = cs[TM - 1:TM, :]

    c3 = _split3(cs)
    aug = _dot(c3[0], place_ref[0]) + _dot(c3[1], place_ref[1]) + _dot(c3[2], place_ref[2])

    lane = lax.broadcasted_iota(jnp.int32, (TM, LANES), 1)
    low = lane < HEAD_DIM
    qt = (q * (HEAD_DIM ** -0.5 * LOG2E)).T
    vt = v.T
    row = lax.broadcasted_iota(jnp.int32, (HEAD_DIM, TM), 0)
    minus_ones = jnp.where(row < 3, -1.0, 0.0).astype(BF16)
    ones_row = jnp.where(lax.broadcasted_iota(jnp.int32, (VROWS - HEAD_DIM, TM), 0) == 0, 1.0, 0.0).astype(BF16)
    for h in range(N_HEADS):
        kp = k[:, (h // 2) * LANES:(h // 2 + 1) * LANES]
        own = low if h % 2 == 0 else jnp.logical_not(low)
        kaug_ref[h] = (jnp.where(own, kp, 0.0) + aug[:, h * LANES:(h + 1) * LANES]).astype(BF16)
        qh = qt[h * HEAD_DIM:(h + 1) * HEAD_DIM, :].astype(BF16)
        if h % 2 == 0:
            qt_ref[h, 0:HEAD_DIM, :] = qh
            qt_ref[h, HEAD_DIM:2 * HEAD_DIM, :] = minus_ones
        else:
            qt_ref[h, 0:HEAD_DIM, :] = minus_ones
            qt_ref[h, HEAD_DIM:2 * HEAD_DIM, :] = qh
        vt_ref[h, 0:HEAD_DIM, :] = vt[h * HEAD_DIM:(h + 1) * HEAD_DIM, :].astype(BF16)
        vt_ref[h, HEAD_DIM:VROWS, :] = ones_row


def _inproj_sample_kernel(x_ref, g_ref, w_ref, bf_ref, q_ref, k_ref, v_ref, lf_ref, lft_ref):
    q, k, v, lf = _inproj_common(x_ref, g_ref, w_ref, bf_ref)
    q_ref[...] = q
    k_ref[...] = k
    v_ref[...] = v
    lf_ref[...] = lf[:, 0:N_HEADS]
    lft_ref[...] = lf.T[0:N_HEADS, :]


def _full(shape):
    n = len(shape)
    return pl.BlockSpec(shape, lambda *_: (0,) * n)


def _inproj_prompt(hp, g1, w_qkvf, b_f, tri, place, seq_len):
    tpad = hp.shape[0]
    nt = tpad // TM
    d = hp.shape[1]
    aw = N_HEADS * HEAD_DIM
    row = lambda i: (i, 0)
    return pl.pallas_call(
        _inproj_prompt_kernel,
        grid=(nt,),
        in_specs=[pl.BlockSpec((TM, d), row), _full(g1.shape), _full(w_qkvf.shape), _full(b_f.shape),
                  _full(tri.shape), _full(place.shape)],
        out_specs=[pl.BlockSpec((TM, aw), row), pl.BlockSpec((TM, aw), row), pl.BlockSpec((TM, N_HEADS), row),
                   pl.BlockSpec((N_HEADS, LANES, TM), lambda i: (0, 0, i)),
                   pl.BlockSpec((N_HEADS, TM, LANES), lambda i: (0, i, 0)),
                   pl.BlockSpec((N_HEADS, VROWS, TM), lambda i: (0, 0, i)),
                   pl.BlockSpec((1, 1, LANES), lambda i: (i, 0, 0))],
        out_shape=[jax.ShapeDtypeStruct((seq_len, aw), F32), jax.ShapeDtypeStruct((seq_len, aw), F32),
                   jax.ShapeDtypeStruct((seq_len, N_HEADS), F32),
                   jax.ShapeDtypeStruct((N_HEADS, LANES, tpad), BF16),
                   jax.ShapeDtypeStruct((N_HEADS, tpad, LANES), BF16),
                   jax.ShapeDtypeStruct((N_HEADS, VROWS, tpad), BF16),
                   jax.ShapeDtypeStruct((nt, 1, LANES), F32)],
        compiler_params=pltpu.CompilerParams(dimension_semantics=("arbitrary",), vmem_limit_bytes=VMEM_LIMIT),
        name="inproj_prompt",
    )(hp, g1, w_qkvf, b_f, tri, place)


def _inproj_sample(xs, g1, w_qkvf, b_f):
    n, d = xs.shape
    nt = n // TM
    aw = N_HEADS * HEAD_DIM
    row = lambda i: (i, 0)
    return pl.pallas_call(
        _inproj_sample_kernel,
        grid=(nt,),
        in_specs=[pl.BlockSpec((TM, d), row), _full(g1.shape), _full(w_qkvf.shape), _full(b_f.shape)],
        out_specs=[pl.BlockSpec((TM, aw), row)] * 3 + [pl.BlockSpec((TM, N_HEADS), row),
                                                        pl.BlockSpec((N_HEADS, TM), lambda i: (0, i))],
        out_shape=[jax.ShapeDtypeStruct((n, aw), F32)] * 3 + [jax.ShapeDtypeStruct((n, N_HEADS), F32),
                                                               jax.ShapeDtypeStruct((N_HEADS, n), F32)],
        compiler_params=pltpu.CompilerParams(dimension_semantics=("arbitrary",), vmem_limit_bytes=VMEM_LIMIT),
        name="inproj_sample",
    )(xs, g1, w_qkvf, b_f)


def _flash_kernel(qi_ref, kj_ref, tot_ref, qt_ref, kaug_ref, vt_ref, o_ref, m_ref, acc_ref, z_ref, extra=()):
    p = pl.program_id(0)
    i = qi_ref[p]
    j = kj_ref[p]

    @pl.when(j == 0)
    def _():
        m_ref[...] = jnp.full(m_ref.shape, NEG_BIG, F32)
        acc_ref[...] = jnp.zeros(acc_ref.shape, F32)

    def scores(h):
        z_ref[h % SCORE_SLOTS] = _dot(kaug_ref[h], qt_ref[h])

    def softmax_pv(h, diagonal):
        z = z_ref[h % SCORE_SLOTS]
        if diagonal:
            s_pos = lax.broadcasted_iota(jnp.int32, (BK, BQ), 0)
            t_pos = lax.broadcasted_iota(jnp.int32, (BK, BQ), 1)
            z = jnp.where(s_pos <= t_pos, z, NEG_BIG)
        m = m_ref[h]
        m_new = jnp.maximum(m, jnp.max(z, axis=0, keepdims=True))
        pexp = jnp.exp2(z - m_new)
        alpha = jnp.exp2(m - m_new)
        acc_ref[h] = alpha * acc_ref[h] + _dot(vt_ref[h], pexp.astype(BF16))
        m_ref[h] = m_new + tot_ref[j * N_HEADS + h]

    def tile(diagonal):
        for h in range(SCORE_SLOTS - 1):
            scores(h)
        for h in range(N_HEADS):
            if h + SCORE_SLOTS - 1 < N_HEADS:
                scores(h + SCORE_SLOTS - 1)
            softmax_pv(h, diagonal)
            if h % 2 == 0 and h // 2 < len(extra):
                extra[h // 2]()

    @pl.when(j < i)
    def _():
        tile(False)

    @pl.when(j == i)
    def _():
        tile(True)
        outs = []
        for h in range(N_HEADS):
            a = acc_ref[h]
            outs.append(a[0:HEAD_DIM, :] / a[HEAD_DIM:HEAD_DIM + 1, :])
        o_ref[...] = jnp.concatenate(outs, axis=0).T.astype(BF16)


SUFFIX_ROWS = 2048


def _suffix_kernel(x_ref, m_ref, o_ref):
    o_ref[...] = _dot3(_split3(x_ref[...]), m_ref[...])


def _suffix(lf_rows, mat):
    n = lf_rows.shape[0]
    return pl.pallas_call(
        _suffix_kernel,
        grid=(n // SUFFIX_ROWS,),
        in_specs=[pl.BlockSpec((SUFFIX_ROWS, LANES), lambda i: (i, 0)), _full(mat.shape)],
        out_specs=pl.BlockSpec((SUFFIX_ROWS, 2 * LANES), lambda i: (i, 0)),
        out_shape=jax.ShapeDtypeStruct((n, 2 * LANES), F32),
        compiler_params=pltpu.CompilerParams(dimension_semantics=("arbitrary",), vmem_limit_bytes=VMEM_LIMIT),
        name="suffix_pages",
    )(lf_rows, mat)


def _page_copies(step, slot, pt_ref, kt_hbm, vt_hbm, sfx_hbm, kbuf, vbuf, sbuf, sems, n_pages, n_chunks):
    ppc = n_pages // n_chunks
    b = step // n_chunks
    c = step % n_chunks
    copies = []
    for jj in range(ppc):
        pg = pt_ref[b * n_pages + n_pages - 1 - (c * ppc + jj)]
        copies.append(pltpu.make_async_copy(kt_hbm.at[pg], kbuf.at[slot, jj], sems.at[0, slot]))
        copies.append(pltpu.make_async_copy(vt_hbm.at[pg], vbuf.at[slot, jj], sems.at[1, slot]))
        copies.append(pltpu.make_async_copy(sfx_hbm.at[pg], sbuf.at[slot, jj], sems.at[2, slot]))
    return copies


def _head_mask():
    aw = N_HEADS * HEAD_DIM
    lane_head = lax.broadcasted_iota(jnp.int32, (N_HEADS, aw), 1) // HEAD_DIM
    return lane_head == lax.broadcasted_iota(jnp.int32, (N_HEADS, aw), 0)


def _decode_init(q_ref, kn_ref, vn_ref, lft_ref, tri_ref, qbd_ref, m_ref, l_ref, acc_ref, carry_ref):
    t_new = q_ref.shape[0]
    aw = N_HEADS * HEAD_DIM
    n_rows = t_new * N_HEADS
    head_mask = _head_mask()
    q = q_ref[...] * (HEAD_DIM ** -0.5)
    qbd = jnp.concatenate(
        [jnp.where(head_mask, jnp.broadcast_to(q[t:t + 1, :], (N_HEADS, aw)), 0.0) for t in range(t_new)], axis=0)
    qbd_ref[...] = qbd.astype(BF16)
    carry_ref[...] = jnp.zeros(carry_ref.shape, F32)

    cnew = _dot3(_split3(lft_ref[...]), tri_ref[...])
    cnew = jnp.concatenate([cnew] * t_new, axis=0)
    t_of_row = lax.broadcasted_iota(jnp.int32, (n_rows, 1), 0) // N_HEADS
    kn = kn_ref[...]
    vn = vn_ref[...]
    zs = []
    for t2 in range(t_new):
        zc = jnp.sum(qbd * kn[t2:t2 + 1, :], axis=-1, keepdims=True)
        zc = zc - cnew[:, t2:t2 + 1]
        zs.append(jnp.where(t_of_row >= t2, zc, NEG_BIG))
    m = functools.reduce(jnp.maximum, zs)
    l = jnp.zeros((n_rows, 1), F32)
    acc = jnp.zeros((n_rows, aw), F32)
    for t2 in range(t_new):
        pe = jnp.exp(zs[t2] - m)
        l = l + pe
        acc = acc + pe * vn[t2:t2 + 1, :]
    m_ref[...] = m
    l_ref[...] = l
    acc_ref[...] = acc


def _decode_chunk_stages(slot, qbd_ref, m_ref, l_ref, acc_ref, carry_ref, kbuf, vbuf, sbuf, zd_ref, pd_ref, al_ref):
    ppc = kbuf.shape[1]
    page = kbuf.shape[3]
    t_new = qbd_ref.shape[0] // N_HEADS

    def scores():
        carry = carry_ref[...]
        biases = []
        for jj in range(ppc):
            sfx = sbuf[slot, jj]
            biases.append(sfx[:, 0:page] + carry)
            carry = carry + sfx[:, LANES:LANES + page]
        carry_ref[...] = carry
        bias = jnp.concatenate(biases, axis=1)
        ktc = jnp.concatenate([kbuf[slot, jj].astype(BF16) for jj in range(ppc)], axis=1)
        zd_ref[...] = _dot(qbd_ref[...], ktc) + jnp.concatenate([bias] * t_new, axis=0)

    def softmax():
        z = zd_ref[...]
        m_prev = m_ref[...]
        m_new = jnp.maximum(m_prev, jnp.max(z, axis=-1, keepdims=True))
        pexp = jnp.exp(z - m_new)
        alpha = jnp.exp(m_prev - m_new)
        l_ref[...] = alpha * l_ref[...] + jnp.sum(pexp, axis=-1, keepdims=True)
        pd_ref[...] = pexp.astype(BF16)
        al_ref[...] = alpha
        m_ref[...] = m_new

    def values():
        vtc = jnp.concatenate([vbuf[slot, jj].astype(BF16) for jj in range(ppc)], axis=1)
        upd = lax.dot_general(pd_ref[...], vtc, (((1,), (1,)), ((), ())), preferred_element_type=F32)
        acc_ref[...] = al_ref[...] * acc_ref[...] + upd

    return [scores, softmax, values]


def _decode_final(o_ref, l_ref, acc_ref):
    t_new = o_ref.shape[0]
    head_mask = _head_mask()
    a = acc_ref[...] / l_ref[...]
    outs = []
    for t in range(t_new):
        blk = jnp.where(head_mask, a[t * N_HEADS:(t + 1) * N_HEADS, :], 0.0)
        outs.append(jnp.sum(blk, axis=0, keepdims=True))
    o_ref[...] = jnp.concatenate(outs, axis=0).astype(o_ref.dtype)


def _attention_kernel(qi_ref, kj_ref, tot_ref, pt_ref,
                      qt_ref, kaug_ref, vt_ref, q_ref, kn_ref, vn_ref, lft_ref, tri_ref, kt_hbm, vt_hbm, sfx_hbm,
                      o_ref, os_ref,
                      fm_ref, facc_ref, z_ref,
                      qbd_ref, dm_ref, dl_ref, dacc_ref, carry_ref, kbuf, vbuf, sbuf, zd_ref, pd_ref, al_ref, sems,
                      *, n_pages, n_chunks, n_dec, n_pairs):
    p = pl.program_id(0)

    def copies(step):
        return _page_copies(step, step % 2, pt_ref, kt_hbm, vt_hbm, sfx_hbm, kbuf, vbuf, sbuf, sems,
                            n_pages, n_chunks)

    @pl.when(p == 0)
    def _():
        for cp in copies(p):
            cp.start()

    @pl.when(p + 1 < n_dec)
    def _():
        for cp in copies(p + 1):
            cp.start()

    decoding = p < n_dec
    c = p % n_chunks

    @pl.when(decoding)
    def _():
        for cp in copies(p):
            cp.wait()

    @pl.when(decoding & (c == 0))
    def _():
        _decode_init(q_ref, kn_ref, vn_ref, lft_ref, tri_ref, qbd_ref, dm_ref, dl_ref, dacc_ref, carry_ref)

    stages = _decode_chunk_stages(p % 2, qbd_ref, dm_ref, dl_ref, dacc_ref, carry_ref, kbuf, vbuf, sbuf,
                                  zd_ref, pd_ref, al_ref)

    @pl.when(p < n_pairs)
    def _():
        _flash_kernel(qi_ref, kj_ref, tot_ref, qt_ref, kaug_ref, vt_ref, o_ref, fm_ref, facc_ref, z_ref, extra=stages)

    @pl.when((p >= n_pairs) & decoding)
    def _():
        for stage in stages:
            stage()

    @pl.when(decoding & (c == n_chunks - 1))
    def _():
        _decode_final(os_ref, dl_ref, dacc_ref)


def _attention(qt, kaug, vt, tot_flat, page_table, q_s, k_new, v_new, lft_pad, tri_new, kt_pages, vt_pages, sfx_pages):
    tpad = qt.shape[2]
    nq = tpad // BQ
    pairs = [(i, j) for i in range(nq) for j in range(i + 1)]
    aw = N_HEADS * HEAD_DIM
    db, t_new, _ = q_s.shape
    n_pages = page_table.shape[1]
    page = kt_pages.shape[2]
    ppc = min(PAGES_PER_STEP, n_pages)
    n_chunks = n_pages // ppc
    n_dec = db * n_chunks
    n_pairs = len(pairs)
    n_steps = max(n_pairs, n_dec)
    pairs = pairs + [pairs[-1]] * (n_steps - n_pairs)
    qi = jnp.asarray([p[0] for p in pairs], jnp.int32)
    kj = jnp.asarray([p[1] for p in pairs], jnp.int32)
    n_rows = t_new * N_HEADS

    seq3 = lambda p, qi, kj, tot, pt: (jnp.minimum(p // n_chunks, db - 1), 0, 0)
    grid_spec = pltpu.PrefetchScalarGridSpec(
        num_scalar_prefetch=4,
        grid=(n_steps,),
        in_specs=[pl.BlockSpec((N_HEADS, LANES, BQ), lambda p, qi, kj, tot, pt: (0, 0, qi[p])),
                  pl.BlockSpec((N_HEADS, BK, LANES), lambda p, qi, kj, tot, pt: (0, kj[p], 0)),
                  pl.BlockSpec((N_HEADS, VROWS, BK), lambda p, qi, kj, tot, pt: (0, 0, kj[p])),
                  pl.BlockSpec((None, t_new, aw), seq3),
                  pl.BlockSpec((None, t_new, aw), seq3),
                  pl.BlockSpec((None, t_new, aw), seq3),
                  pl.BlockSpec((None, N_HEADS, LANES), seq3),
                  pl.BlockSpec(tri_new.shape, lambda p, qi, kj, tot, pt: (0, 0)),
                  pl.BlockSpec(memory_space=pl.ANY),
                  pl.BlockSpec(memory_space=pl.ANY),
                  pl.BlockSpec(memory_space=pl.ANY)],
        out_specs=[pl.BlockSpec((BQ, aw), lambda p, qi, kj, tot, pt: (qi[p], 0)),
                   pl.BlockSpec((None, t_new, aw), seq3)],
        scratch_shapes=[pltpu.VMEM((N_HEADS, 1, BQ), F32), pltpu.VMEM((N_HEADS, VROWS, BQ), F32),
                        pltpu.VMEM((SCORE_SLOTS, BK, BQ), F32),
                        pltpu.VMEM((n_rows, aw), BF16), pltpu.VMEM((n_rows, 1), F32), pltpu.VMEM((n_rows, 1), F32),
                        pltpu.VMEM((n_rows, aw), F32), pltpu.VMEM((N_HEADS, LANES), F32),
                        pltpu.VMEM((2, ppc, aw, page), F32), pltpu.VMEM((2, ppc, aw, page), F32),
                        pltpu.VMEM((2, ppc, N_HEADS, 2 * LANES), F32),
                        pltpu.VMEM((n_rows, ppc * page), F32), pltpu.VMEM((n_rows, ppc * page), BF16),
                        pltpu.VMEM((n_rows, 1), F32), pltpu.SemaphoreType.DMA((3, 2))],
    )
    kern = functools.partial(_attention_kernel, n_pages=n_pages, n_chunks=n_chunks, n_dec=n_dec, n_pairs=n_pairs)
    return pl.pallas_call(
        kern,
        grid_spec=grid_spec,
        out_shape=[jax.ShapeDtypeStruct((tpad, aw), BF16), jax.ShapeDtypeStruct((db, t_new, aw), BF16)],
        compiler_params=pltpu.CompilerParams(dimension_semantics=("arbitrary",), vmem_limit_bytes=VMEM_LIMIT),
        name="attention",
    )(qi, kj, tot_flat, page_table.reshape(-1), qt, kaug, vt, q_s, k_new, v_new, lft_pad, tri_new,
      kt_pages, vt_pages, sfx_pages)


def _mix_and_route(h, xn, a_bf, d_groups, w_ugg_ga_gb, poolw_ref, pscale_ref, wba_ref, wbp_ref, wout_ref,
                   g2_ref, wrt_ref, br_ref, h1_ref, xn2_ref, comb_ref, grow_ref):
    ga, gb = w_ugg_ga_gb
    mixed = [_dot(d_groups[g].astype(BF16), poolw_ref[g]) for g in range(len(POOL_WINDOWS))]
    pooled = jnp.concatenate(mixed, axis=-1) * pscale_ref[...]
    m = jax.nn.sigmoid(ga) * _dot(a_bf, wba_ref[...]) + jax.nn.sigmoid(gb) * _dot(pooled.astype(BF16), wbp_ref[...])
    h1 = h + _dot(m.astype(BF16), wout_ref[...])
    h1_ref[...] = h1
    xn2 = _rmsnorm(h1, g2_ref[...])
    xn2_bf = xn2.astype(BF16)
    xn2_ref[...] = xn2_bf

    lt = lax.dot_general(wrt_ref[...], xn2_bf, (((1,), (1,)), ((), ())), preferred_element_type=F32) + br_ref[...]
    g_rows = [lt[g:g + 1, :] for g in range(N_EXPERT_GROUPS)]
    gmax = functools.reduce(jnp.maximum, g_rows)
    gsum = functools.reduce(lambda a, b: a + b, [jnp.exp(r - gmax) for r in g_rows])
    pg_sel = 1.0 / gsum
    gsel = jnp.full(gmax.shape, N_EXPERT_GROUPS - 1, jnp.int32)
    for g in range(N_EXPERT_GROUPS - 2, -1, -1):
        gsel = jnp.where(g_rows[g] == gmax, g, gsel)
    e_rows = []
    for k in range(EXPERTS_PER_GROUP):
        r = lt[N_EXPERT_GROUPS + k:N_EXPERT_GROUPS + k + 1, :]
        for g in range(1, N_EXPERT_GROUPS):
            base = N_EXPERT_GROUPS + g * EXPERTS_PER_GROUP + k
            r = jnp.where(gsel == g, lt[base:base + 1, :], r)
        e_rows.append(r)
    v1 = functools.reduce(jnp.maximum, e_rows)
    i1 = jnp.full(gsel.shape, EXPERTS_PER_GROUP - 1, jnp.int32)
    for k in range(EXPERTS_PER_GROUP - 2, -1, -1):
        i1 = jnp.where(e_rows[k] == v1, k, i1)
    rest = [jnp.where(i1 == k, -jnp.inf, e_rows[k]) for k in range(EXPERTS_PER_GROUP)]
    v2 = functools.reduce(jnp.maximum, rest)
    i2 = jnp.full(gsel.shape, EXPERTS_PER_GROUP - 1, jnp.int32)
    for k in range(EXPERTS_PER_GROUP - 2, -1, -1):
        i2 = jnp.where((rest[k] == v2) & (i1 != k), k, i2)
    e2 = jnp.exp(v2 - v1)
    w1 = (1.0 / (1.0 + e2)) * pg_sel
    w2 = (e2 / (1.0 + e2)) * pg_sel
    id1 = gsel * EXPERTS_PER_GROUP + i1
    id2 = gsel * EXPERTS_PER_GROUP + i2
    ntok = lt.shape[1]
    erow = lax.broadcasted_iota(jnp.int32, (LANES, ntok), 0)
    comb_t = jnp.where(erow == id1, w1, 0.0) + jnp.where(erow == id2, w2, 0.0)
    comb_t = comb_t + jnp.where(erow == N_EXPERTS + gsel, 1.0, 0.0)
    comb_ref[...] = comb_t.T
    grow_ref[...] = jnp.where(lax.broadcasted_iota(jnp.int32, (8, ntok), 0) == gsel, 1.0, 0.0)


def _merge_prompt_kernel(h_ref, a_ref, g1_ref, wugg_ref, poolw_ref, pscale_ref, wba_ref, wbp_ref, wout_ref,
                         g2_ref, wrt_ref, br_ref, h1_ref, xn2_ref, comb_ref, grow_ref, utail_ref, ext_ref, tail_ref):
    i = pl.program_id(0)

    @pl.when(i == pl.num_programs(0) - 1)
    def _():
        h1_ref[...] = jnp.zeros(h1_ref.shape, h1_ref.dtype)
        xn2_ref[...] = jnp.zeros(xn2_ref.shape, xn2_ref.dtype)
        comb_ref[...] = jnp.zeros(comb_ref.shape, comb_ref.dtype)
        grow_ref[...] = jnp.zeros(grow_ref.shape, grow_ref.dtype)

    @pl.when(i < pl.num_programs(0) - 1)
    def _():
        _merge_prompt_tile(i, h_ref, a_ref, g1_ref, wugg_ref, poolw_ref, pscale_ref, wba_ref, wbp_ref, wout_ref,
                           g2_ref, wrt_ref, br_ref, h1_ref, xn2_ref, comb_ref, grow_ref, utail_ref, ext_ref, tail_ref)


def _merge_prompt_tile(i, h_ref, a_ref, g1_ref, wugg_ref, poolw_ref, pscale_ref, wba_ref, wbp_ref, wout_ref,
                       g2_ref, wrt_ref, br_ref, h1_ref, xn2_ref, comb_ref, grow_ref, utail_ref, ext_ref, tail_ref):
    pw = LANES * len(POOL_WINDOWS)
    hist_rows = 16

    @pl.when(i == 0)
    def _():
        tail_ref[...] = jnp.zeros(tail_ref.shape, F32)

    h = h_ref[...]
    xn = _rmsnorm(h, g1_ref[...]).astype(BF16)
    ugg = _dot(xn, wugg_ref[...])
    d = h.shape[1]
    u = ugg[:, 0:pw]
    ga = ugg[:, pw:pw + d]
    gb = ugg[:, pw + d:pw + 2 * d]
    utail_ref[...] = u[0:hist_rows, :]

    ext_ref[0:hist_rows, :] = tail_ref[...]
    ext_ref[hist_rows:hist_rows + TM, :] = u
    tail_ref[...] = u[TM - hist_rows:TM, :]

    pos = lax.broadcasted_iota(jnp.int32, (TM, LANES), 0) + i * TM
    d_groups = []
    for g, w in enumerate(POOL_WINDOWS):
        lanes = slice(g * LANES, (g + 1) * LANES)
        tok = ext_ref[hist_rows:hist_rows + TM, lanes]
        acc = tok
        for back in range(1, w):
            acc = acc + ext_ref[hist_rows - back:hist_rows - back + TM, lanes]
        cnt = jnp.minimum(pos + 1, w).astype(F32)
        d_groups.append(acc / cnt - tok)

    _mix_and_route(h, xn, a_ref[...], d_groups, (ga, gb), poolw_ref, pscale_ref, wba_ref, wbp_ref, wout_ref,
                   g2_ref, wrt_ref, br_ref, h1_ref, xn2_ref, comb_ref, grow_ref)


def _merge_sample_kernel(h_ref, a_ref, hist_ref, g1_ref, wugg_ref, poolw_ref, pscale_ref, wba_ref, wbp_ref, wout_ref,
                         g2_ref, wrt_ref, br_ref, h1_in, xn2_in, comb_in, grow_in, h1_ref, xn2_ref, comb_ref, grow_ref, u_ref,
                         *, pos0, db):
    pw = LANES * len(POOL_WINDOWS)
    h = h_ref[...]
    xn = _rmsnorm(h, g1_ref[...]).astype(BF16)
    ugg = _dot(xn, wugg_ref[...])
    d = h.shape[1]
    u = ugg[:, 0:pw]
    ga = ugg[:, pw:pw + d]
    gb = ugg[:, pw + d:pw + 2 * d]
    u_ref[...] = u
    t_new = h.shape[0] // db
    n_hist = hist_ref.shape[0]

    def ext(e, lanes):
        if e < n_hist:
            return hist_ref[e][:, lanes]
        return u[(e - n_hist) * db:(e - n_hist + 1) * db, lanes]

    d_groups = []
    for g, w in enumerate(POOL_WINDOWS):
        lanes = slice(g * LANES, (g + 1) * LANES)
        per_t = []
        for t in range(t_new):
            tok = ext(n_hist + t, lanes)
            acc = tok
            for back in range(1, w):
                acc = acc + ext(n_hist + t - back, lanes)
            cnt = float(min(pos0 + t + 1, w))
            per_t.append(acc / cnt - tok)
        d_groups.append(jnp.concatenate(per_t, axis=0))

    _mix_and_route(h, xn, a_ref[...], d_groups, (ga, gb), poolw_ref, pscale_ref, wba_ref, wbp_ref, wout_ref,
                   g2_ref, wrt_ref, br_ref, h1_ref, xn2_ref, comb_ref, grow_ref)


def _merge_weights_specs(ws):
    return [_full(w.shape) for w in ws]


def _merge_prompt(hp, a_bf, ws, n_all):
    tpad, d = hp.shape
    nt = tpad // TM
    assert n_all == (nt + 1) * TM
    aw = a_bf.shape[1]
    pw = LANES * len(POOL_WINDOWS)
    row = lambda i: (i, 0)
    src_row = lambda i: (jnp.minimum(i, nt - 1), 0)
    return pl.pallas_call(
        _merge_prompt_kernel,
        grid=(nt + 1,),
        in_specs=[pl.BlockSpec((TM, d), src_row), pl.BlockSpec((TM, aw), src_row)] + _merge_weights_specs(ws),
        out_specs=[pl.BlockSpec((TM, d), row), pl.BlockSpec((TM, d), row), pl.BlockSpec((TM, LANES), row),
                   pl.BlockSpec((8, TM), lambda i: (0, i)), pl.BlockSpec((16, pw), lambda i: (0, 0))],
        out_shape=[jax.ShapeDtypeStruct((n_all, d), F32), jax.ShapeDtypeStruct((n_all, d), BF16),
                   jax.ShapeDtypeStruct((n_all, LANES), F32), jax.ShapeDtypeStruct((8, n_all), F32),
                   jax.ShapeDtypeStruct((16, pw), F32)],
        scratch_shapes=[pltpu.VMEM((TM + 16, pw), F32), pltpu.VMEM((16, pw), F32)],
        compiler_params=pltpu.CompilerParams(dimension_semantics=("arbitrary",), vmem_limit_bytes=VMEM_LIMIT),
        name="merge_prompt",
    )(hp, a_bf, *ws)


def _merge_sample(xs, a_bf, hist_t, ws, all_arrays, pos0, db):
    n, d = xs.shape
    assert n == TM
    aw = a_bf.shape[1]
    pw = LANES * len(POOL_WINDOWS)
    last = all_arrays[0].shape[0] // TM - 1
    row = lambda i: (i, 0)
    tail = lambda i: (last, 0)
    kern = functools.partial(_merge_sample_kernel, pos0=pos0, db=db)
    n_in = 3 + len(ws)
    return pl.pallas_call(
        kern,
        grid=(1,),
        in_specs=[pl.BlockSpec((TM, d), row), pl.BlockSpec((TM, aw), row), _full(hist_t.shape)]
        + _merge_weights_specs(ws) + [pl.BlockSpec(memory_space=pl.ANY)] * 4,
        out_specs=[pl.BlockSpec((TM, d), tail), pl.BlockSpec((TM, d), tail), pl.BlockSpec((TM, LANES), tail),
                   pl.BlockSpec((8, TM), lambda i: (0, last)), pl.BlockSpec((TM, pw), row)],
        out_shape=[jax.ShapeDtypeStruct(a.shape, a.dtype) for a in all_arrays] + [jax.ShapeDtypeStruct((n, pw), F32)],
        input_output_aliases={n_in + k: k for k in range(4)},
        compiler_params=pltpu.CompilerParams(dimension_semantics=("arbitrary",), vmem_limit_bytes=VMEM_LIMIT),
        name="merge_sample",
    )(xs, a_bf, hist_t, *ws, *all_arrays)


MOE_TM = 1024
MOE_WIN = 320
MOE_ALIGN = 16
MOE_SORTED = MOE_TM + N_EXPERT_GROUPS * MOE_ALIGN
MOE_ROWS = MOE_SORTED + MOE_WIN
MOE_WINDOWS = -(-MOE_TM // MOE_WIN)


def _moe_kernel(x_ref, comb_ref, grow_ref, h1_ref, wg_ref, wu_ref, wd_ref, gf_ref, o_ref,
                xs_ref, cs_ref, acc_ref, pt_ref, meta_ref):
    e = pl.program_id(1)
    tm = x_ref.shape[0]
    lane = lax.broadcasted_iota(jnp.int32, (tm, LANES), 1)

    @pl.when(e == 0)
    def _():
        onehot_rows = grow_ref[...]
        comb = comb_ref[...]
        onehot_cols = jnp.where((lane >= N_EXPERTS) & (lane < N_EXPERTS + N_EXPERT_GROUPS), comb, 0.0)
        r_idx = lax.broadcasted_iota(jnp.int32, (tm, tm), 0)
        c_idx = lax.broadcasted_iota(jnp.int32, (tm, tm), 1)
        upper = jnp.where(r_idx < c_idx, 1.0, 0.0).astype(BF16)
        lower = jnp.where(c_idx < r_idx, 1.0, 0.0).astype(BF16)
        rank_rows = _dot(onehot_rows.astype(BF16), upper)
        rank_cols = _dot(lower, onehot_cols.astype(BF16))
        pos_row = jnp.zeros((1, tm), F32)
        pos_col = jnp.zeros((tm, 1), F32)
        start = jnp.int32(0)
        for g in range(N_EXPERT_GROUPS):
            count = jnp.sum(onehot_rows[g:g + 1, :]).astype(jnp.int32)
            meta_ref[g] = start
            meta_ref[N_EXPERT_GROUPS + g] = count
            startf = start.astype(F32)
            pos_row = pos_row + onehot_rows[g:g + 1, :] * (rank_rows[g:g + 1, :] + startf)
            lg = N_EXPERTS + g
            pos_col = pos_col + onehot_cols[:, lg:lg + 1] * (rank_cols[:, lg:lg + 1] + startf)
            start = start + ((count + (MOE_ALIGN - 1)) // MOE_ALIGN) * MOE_ALIGN
        s_rows = lax.broadcasted_iota(jnp.int32, (MOE_SORTED, tm), 0)
        perm = jnp.where(s_rows == pos_row.astype(jnp.int32), 1.0, 0.0).astype(BF16)
        s_cols = lax.broadcasted_iota(jnp.int32, (tm, MOE_SORTED), 1)
        pt_ref[...] = jnp.where(s_cols == pos_col.astype(jnp.int32), 1.0, 0.0).astype(BF16)
        xs_ref[0:MOE_SORTED, :] = _dot(perm, x_ref[...]).astype(BF16)
        xs_ref[MOE_SORTED:MOE_ROWS, :] = jnp.zeros((MOE_ROWS - MOE_SORTED, x_ref.shape[1]), BF16)
        cs_ref[0:MOE_SORTED, :] = _dot3_left(perm, _split3(comb))
        cs_ref[MOE_SORTED:MOE_ROWS, :] = jnp.zeros((MOE_ROWS - MOE_SORTED, LANES), F32)
        acc_ref[...] = jnp.zeros(acc_ref.shape, F32)

    g = e // EXPERTS_PER_GROUP
    start = meta_ref[g]
    count = meta_ref[N_EXPERT_GROUPS + g]
    wlane = lax.broadcasted_iota(jnp.int32, (MOE_WIN, LANES), 1)
    for w in range(MOE_WINDOWS):
        @pl.when(w * MOE_WIN < count)
        def _():
            rows = pl.ds(pl.multiple_of(start + w * MOE_WIN, MOE_ALIGN), MOE_WIN)
            x = xs_ref[rows, :]
            hdn = jax.nn.silu(_dot(x, wg_ref[...])) * _dot(x, wu_ref[...])
            out = _dot(hdn.astype(BF16), wd_ref[...])
            col = jnp.sum(jnp.where(wlane == e, cs_ref[rows, :], 0.0), axis=-1, keepdims=True)
            acc_ref[rows, :] += col * out

    @pl.when(e == pl.num_programs(1) - 1)
    def _():
        y = acc_ref[0:MOE_SORTED, :]
        hi = y.astype(BF16)
        lo = (y - hi.astype(F32)).astype(BF16)
        back = pt_ref[...]
        o_ref[...] = _rmsnorm(h1_ref[...] + _dot(back, hi) + _dot(back, lo), gf_ref[...])


def _moe(xn2, comb, grow, h1, wg, wu, wd, gf):
    n, d = xn2.shape
    nt = n // MOE_TM
    ne, _, de = wg.shape
    row = lambda i, e: (i, 0)
    return pl.pallas_call(
        _moe_kernel,
        grid=(nt, ne),
        in_specs=[pl.BlockSpec((MOE_TM, d), row), pl.BlockSpec((MOE_TM, LANES), row),
                  pl.BlockSpec((8, MOE_TM), lambda i, e: (0, i)), pl.BlockSpec((MOE_TM, d), row),
                  pl.BlockSpec((None, d, de), lambda i, e: (e, 0, 0)),
                  pl.BlockSpec((None, d, de), lambda i, e: (e, 0, 0)),
                  pl.BlockSpec((None, de, d), lambda i, e: (e, 0, 0)),
                  pl.BlockSpec(gf.shape, lambda i, e: (0, 0))],
        out_specs=pl.BlockSpec((MOE_TM, d), row),
        out_shape=jax.ShapeDtypeStruct((n, d), F32),
        scratch_shapes=[pltpu.VMEM((MOE_ROWS, d), BF16), pltpu.VMEM((MOE_ROWS, LANES), F32),
                        pltpu.VMEM((MOE_ROWS, d), F32), pltpu.VMEM((MOE_TM, MOE_SORTED), BF16),
                        pltpu.SMEM((2 * N_EXPERT_GROUPS,), jnp.int32)],
        compiler_params=pltpu.CompilerParams(dimension_semantics=("arbitrary", "arbitrary"),
                                             vmem_limit_bytes=VMEM_LIMIT),
        name="moe_grouped",
    )(xn2, comb, grow, h1, wg, wu, wd, gf)


def _placement_matrices():
    import numpy as np
    m = np.zeros((3, LANES, N_HEADS * LANES), np.float32)
    for h in range(N_HEADS):
        base = h * LANES + (HEAD_DIM if h % 2 == 0 else 0)
        for j in range(3):
            m[j, h, base + j] = 1.0
    return jnp.asarray(m, BF16)


def kernel(x_prompt, x_sample, cache_k, cache_v, cache_logf, state_pool, page_table, meta_tokens, norm1_g, w_in,
           b_forget, pool_w, pool_scale, w_branch_attn, w_branch_pool, w_out, norm2_g, w_router_group,
           b_router_group, w_router_expert, b_router_expert, w_gate, w_up, w_down, norm_f_g):
    import numpy as np
    depth = w_in.shape[0]
    assert depth == 1
    batch, seq, d = x_prompt.shape
    assert batch == 1
    db, t_new, _ = x_sample.shape
    assert db * t_new == TM
    aw = N_HEADS * HEAD_DIM
    pw = LANES * len(POOL_WINDOWS)
    seq_len = seq + N_META
    tpad = -(-seq_len // TM) * TM
    assert (seq_len - 16) % TM == 0 and tpad > seq_len
    n_pool, page = cache_k.shape[1], cache_k.shape[2]
    n_pages = page_table.shape[1]
    past_len = n_pages * page
    assert page == LANES and t_new <= 8

    wl = w_in[0]
    w_qkvf = jnp.concatenate([wl[:, 0:3 * aw], jnp.pad(wl[:, 3 * aw:3 * aw + N_HEADS], ((0, 0), (0, LANES - N_HEADS)))],
                             axis=1).astype(BF16)
    w_ugg = wl[:, 3 * aw + N_HEADS:].astype(BF16)
    b_f = jnp.pad(b_forget[0], (0, LANES - N_HEADS)).reshape(1, LANES)
    g1 = norm1_g[0].reshape(1, d)
    g2 = norm2_g[0].reshape(1, d)
    gf = norm_f_g.reshape(1, d)
    n_r = N_EXPERT_GROUPS + N_EXPERTS
    w_rt = jnp.pad(jnp.concatenate([w_router_group[0], w_router_expert[0]], axis=1).T, ((0, 32 - n_r), (0, 0))).astype(BF16)
    b_r = jnp.pad(jnp.concatenate([b_router_group[0], b_router_expert[0]]), (0, 32 - n_r))
    b_r = jnp.broadcast_to(b_r[:, None], (32, TM))
    merge_ws = [g1, w_ugg, pool_w[0].astype(BF16), pool_scale[0].reshape(1, pw), w_branch_attn[0].astype(BF16),
                w_branch_pool[0].astype(BF16), w_out[0].astype(BF16), g2, w_rt, b_r]
    wg = w_gate[0].astype(BF16)
    wu = w_up[0].astype(BF16)
    wd = w_down[0].astype(BF16)

    tri_tm = jnp.asarray(np.tril(np.ones((TM, TM), np.float32)), BF16)
    place = _placement_matrices()

    hp = jnp.concatenate([meta_tokens.astype(x_prompt.dtype), x_prompt[0],
                          jnp.zeros((tpad - seq_len, d), x_prompt.dtype)], axis=0)
    k_p, v_p, lf_p, qt, kaug, vt, tot = _inproj_prompt(hp, g1, w_qkvf, b_f, tri_tm, place, seq_len)

    xs = jnp.transpose(x_sample, (1, 0, 2)).reshape(TM, d)
    q_s, k_s, v_s, lf_s, lft_s = _inproj_sample(xs, g1, w_qkvf, b_f)
    to_seq = lambda z: jnp.transpose(z.reshape(t_new, db, -1), (1, 0, 2))
    k_seq, v_seq = to_seq(k_s), to_seq(v_s)
    lft_pad = jnp.pad(jnp.transpose(lft_s.reshape(N_HEADS, t_new, db), (2, 0, 1)), ((0, 0), (0, 0), (0, LANES - t_new)))
    tri_new = jnp.asarray(np.triu(np.ones((LANES, LANES), np.float32)) * (np.arange(LANES) < t_new)[None, :], BF16)

    lf_rows = jnp.transpose(cache_logf[0], (0, 2, 1)).reshape(n_pool * N_HEADS, page)
    sfx_mat = jnp.asarray(np.concatenate([np.tril(np.ones((page, page), np.float32), -1),
                                          np.ones((page, page), np.float32)], axis=1), BF16)
    rows_pad = -(-lf_rows.shape[0] // SUFFIX_ROWS) * SUFFIX_ROWS
    sfx = _suffix(jnp.pad(lf_rows, ((0, rows_pad - lf_rows.shape[0]), (0, 0))), sfx_mat)
    sfx_pages = sfx[:n_pool * N_HEADS].reshape(n_pool, N_HEADS, 2 * LANES)

    kt_pages = jnp.transpose(cache_k[0], (0, 2, 3, 1)).reshape(n_pool, aw, page)
    vt_pages = jnp.transpose(cache_v[0], (0, 2, 3, 1)).reshape(n_pool, aw, page)
    a_p, a_s = _attention(qt, kaug, vt, tot[:, 0, 0:N_HEADS].reshape(-1), page_table, to_seq(q_s), k_seq, v_seq,
                          lft_pad, tri_new, kt_pages, vt_pages, sfx_pages)
    n_all = tpad + TM
    assert n_all % MOE_TM == 0
    h1_a, xn2_a, comb_a, grow_a, utail = _merge_prompt(hp, a_p, merge_ws, n_all)
    a_s_t = jnp.transpose(a_s, (1, 0, 2)).reshape(TM, aw)
    hist_t = jnp.transpose(state_pool[0], (1, 0, 2))
    h1_a, xn2_a, comb_a, grow_a, u_s = _merge_sample(xs, a_s_t, hist_t, merge_ws, (h1_a, xn2_a, comb_a, grow_a),
                                                     past_len, db)
    y_a = _moe(xn2_a, comb_a, grow_a, h1_a, wg, wu, wd, gf)

    y_prompt = y_a[N_META:seq_len].reshape(1, seq, d)
    y_sample = to_seq(y_a[tpad:])
    k_prompt = k_p.reshape(1, 1, seq_len, N_HEADS, HEAD_DIM)
    v_prompt = v_p.reshape(1, 1, seq_len, N_HEADS, HEAD_DIM)
    logf_prompt = lf_p.reshape(1, 1, seq_len, N_HEADS)
    pool_prompt = utail[16 - POOL_HIST:16].reshape(1, 1, POOL_HIST, pw)
    k_sample = k_seq.reshape(1, db, t_new, N_HEADS, HEAD_DIM)
    v_sample = v_seq.reshape(1, db, t_new, N_HEADS, HEAD_DIM)
    logf_sample = to_seq(lf_s).reshape(1, db, t_new, N_HEADS)
    pool_sample = jnp.concatenate([state_pool[0].astype(F32), to_seq(u_s)], axis=1)[:, -POOL_HIST:].reshape(
        1, db, POOL_HIST, pw)
    return (y_prompt, y_sample, k_prompt, v_prompt, logf_prompt, pool_prompt,
            k_sample, v_sample, logf_sample, pool_sample)
```

```python
import functools

import jax
import jax.numpy as jnp
from jax import lax
from jax.experimental import pallas as pl
from jax.experimental.pallas import tpu as pltpu

F32 = jnp.float32
BF16 = jnp.bfloat16

N_HEADS = 8
HEAD_DIM = 64
N_META = 16
POOL_WINDOWS = (2, 4, 8, 16)
POOL_HIST = 15
N_EXPERT_GROUPS = 4
EXPERTS_PER_GROUP = 4
N_EXPERTS = 16
RMS_EPS = 1e-6

LANES = 128
TM = 512
BQ = 512
BK = 512
PAGES_PER_STEP = 16
SCORE_SLOTS = 3
VROWS = HEAD_DIM + 16
NEG_BIG = -1e30
FRONT = TM - N_META
LOG2E = 1.4426950408889634
VMEM_LIMIT = 52 * 1024 * 1024


def _rmsnorm(x, g):
    inv = lax.rsqrt(jnp.mean(x * x, axis=-1, keepdims=True) + RMS_EPS)
    return (x * inv) * g


def _log_sigmoid(x):
    return jnp.minimum(x, 0.0) - jnp.log1p(jnp.exp(-jnp.abs(x)))


def _split3(x):
    hi = x.astype(BF16)
    r1 = x - hi.astype(F32)
    mid = r1.astype(BF16)
    lo = (r1 - mid.astype(F32)).astype(BF16)
    return hi, mid, lo


def _dot(a, b):
    return jnp.dot(a, b, preferred_element_type=F32)


def _dot3(parts, m):
    return _dot(parts[0], m) + _dot(parts[1], m) + _dot(parts[2], m)


def _dot3_left(m, parts):
    return _dot(m, parts[0]) + _dot(m, parts[1]) + _dot(m, parts[2])


def _inproj_common(x, g_ref, w_ref, bf_ref):
    aw = N_HEADS * HEAD_DIM
    xn = _rmsnorm(x, g_ref[...]).astype(BF16)
    proj = _dot(xn, w_ref[...])
    q = proj[:, 0:aw]
    k = proj[:, aw:2 * aw]
    v = proj[:, 2 * aw:3 * aw]
    lf = _log_sigmoid(proj[:, 3 * aw:3 * aw + LANES] + bf_ref[...])
    return q, k, v, lf


def _frame_tile(head_ref, x_ref):
    return jnp.where(pl.program_id(0) == 0, head_ref[...], x_ref[...])


def _inproj_prompt_kernel(head_ref, x_ref, g_ref, w_ref, bf_ref, tri_ref, place_ref,
                          k_ref, v_ref, lf_ref, qt_ref, kaug_ref, vt_ref, tot_ref):
    q, k, v, lf = _inproj_common(_frame_tile(head_ref, x_ref), g_ref, w_ref, bf_ref)
    k_ref[...] = k
    v_ref[...] = v
    lf_ref[...] = lf[:, 0:N_HEADS]

    real = lax.broadcasted_iota(jnp.int32, (TM, LANES), 0) + pl.program_id(0) * TM >= FRONT
    cs = _dot3_left(tri_ref[...], _split3(jnp.where(real, lf, 0.0))) * LOG2E
    tot_ref[0] = cs[TM - 1:TM, :]

    c3 = _split3(jnp.where(real, cs, -NEG_BIG))
    aug = _dot(c3[0], place_ref[0]) + _dot(c3[1], place_ref[1]) + _dot(c3[2], place_ref[2])

    lane = lax.broadcasted_iota(jnp.int32, (TM, LANES), 1)
    low = lane < HEAD_DIM
    qt = (q * (HEAD_DIM ** -0.5 * LOG2E)).T
    vt = v.T
    row = lax.broadcasted_iota(jnp.int32, (HEAD_DIM, TM), 0)
    minus_ones = jnp.where(row < 3, -1.0, 0.0).astype(BF16)
    ones_row = jnp.where(lax.broadcasted_iota(jnp.int32, (VROWS - HEAD_DIM, TM), 0) == 0, 1.0, 0.0).astype(BF16)
    for h in range(N_HEADS):
        kp = k[:, (h // 2) * LANES:(h // 2 + 1) * LANES]
        own = low if h % 2 == 0 else jnp.logical_not(low)
        kaug_ref[h] = (jnp.where(own, kp, 0.0) + aug[:, h * LANES:(h + 1) * LANES]).astype(BF16)
        qh = qt[h * HEAD_DIM:(h + 1) * HEAD_DIM, :].astype(BF16)
        if h % 2 == 0:
            qt_ref[h, 0:HEAD_DIM, :] = qh
            qt_ref[h, HEAD_DIM:2 * HEAD_DIM, :] = minus_ones
        else:
            qt_ref[h, 0:HEAD_DIM, :] = minus_ones
            qt_ref[h, HEAD_DIM:2 * HEAD_DIM, :] = qh
        vt_ref[h, 0:HEAD_DIM, :] = vt[h * HEAD_DIM:(h + 1) * HEAD_DIM, :].astype(BF16)
        vt_ref[h, HEAD_DIM:VROWS, :] = ones_row


def _inproj_sample_kernel(x_ref, g_ref, w_ref, bf_ref, q_ref, k_ref, v_ref, lf_ref, lft_ref):
    q, k, v, lf = _inproj_common(x_ref[...], g_ref, w_ref, bf_ref)
    q_ref[...] = q
    k_ref[...] = k
    v_ref[...] = v
    lf_ref[...] = lf[:, 0:N_HEADS]
    lft_ref[...] = lf.T[0:N_HEADS, :]


def _full(shape):
    n = len(shape)
    return pl.BlockSpec(shape, lambda *_: (0,) * n)


def _inproj_prompt(head, x, g1, w_qkvf, b_f, tri, place):
    seq, d = x.shape
    tpad = seq + TM
    nt = tpad // TM
    aw = N_HEADS * HEAD_DIM
    row = lambda i: (i, 0)
    x_row = lambda i: (jnp.maximum(i - 1, 0), 0)
    return pl.pallas_call(
        _inproj_prompt_kernel,
        grid=(nt,),
        in_specs=[_full(head.shape), pl.BlockSpec((TM, d), x_row), _full(g1.shape), _full(w_qkvf.shape),
                  _full(b_f.shape), _full(tri.shape), _full(place.shape)],
        out_specs=[pl.BlockSpec((TM, aw), row), pl.BlockSpec((TM, aw), row), pl.BlockSpec((TM, N_HEADS), row),
                   pl.BlockSpec((N_HEADS, LANES, TM), lambda i: (0, 0, i)),
                   pl.BlockSpec((N_HEADS, TM, LANES), lambda i: (0, i, 0)),
                   pl.BlockSpec((N_HEADS, VROWS, TM), lambda i: (0, 0, i)),
                   pl.BlockSpec((1, 1, LANES), lambda i: (i, 0, 0))],
        out_shape=[jax.ShapeDtypeStruct((tpad, aw), F32), jax.ShapeDtypeStruct((tpad, aw), F32),
                   jax.ShapeDtypeStruct((tpad, N_HEADS), F32),
                   jax.ShapeDtypeStruct((N_HEADS, LANES, tpad), BF16),
                   jax.ShapeDtypeStruct((N_HEADS, tpad, LANES), BF16),
                   jax.ShapeDtypeStruct((N_HEADS, VROWS, tpad), BF16),
                   jax.ShapeDtypeStruct((nt, 1, LANES), F32)],
        compiler_params=pltpu.CompilerParams(dimension_semantics=("arbitrary",), vmem_limit_bytes=VMEM_LIMIT),
        name="inproj_prompt",
    )(head, x, g1, w_qkvf, b_f, tri, place)


def _inproj_sample(xs, g1, w_qkvf, b_f):
    n, d = xs.shape
    nt = n // TM
    aw = N_HEADS * HEAD_DIM
    row = lambda i: (i, 0)
    return pl.pallas_call(
        _inproj_sample_kernel,
        grid=(nt,),
        in_specs=[pl.BlockSpec((TM, d), row), _full(g1.shape), _full(w_qkvf.shape), _full(b_f.shape)],
        out_specs=[pl.BlockSpec((TM, aw), row)] * 3 + [pl.BlockSpec((TM, N_HEADS), row),
                                                        pl.BlockSpec((N_HEADS, TM), lambda i: (0, i))],
        out_shape=[jax.ShapeDtypeStruct((n, aw), F32)] * 3 + [jax.ShapeDtypeStruct((n, N_HEADS), F32),
                                                               jax.ShapeDtypeStruct((N_HEADS, n), F32)],
        compiler_params=pltpu.CompilerParams(dimension_semantics=("arbitrary",), vmem_limit_bytes=VMEM_LIMIT),
        name="inproj_sample",
    )(xs, g1, w_qkvf, b_f)


def _flash_kernel(qi_ref, kj_ref, tot_ref, qt_ref, kaug_ref, vt_ref, o_ref, m_ref, acc_ref, z_ref, extra=()):
    p = pl.program_id(0)
    i = qi_ref[p]
    j = kj_ref[p]

    @pl.when(j == 0)
    def _():
        m_ref[...] = jnp.full(m_ref.shape, NEG_BIG, F32)
        acc_ref[...] = jnp.zeros(acc_ref.shape, F32)

    def scores(h):
        z_ref[h % SCORE_SLOTS] = _dot(kaug_ref[h], qt_ref[h])

    def softmax_pv(h, diagonal):
        z = z_ref[h % SCORE_SLOTS]
        if diagonal:
            s_pos = lax.broadcasted_iota(jnp.int32, (BK, BQ), 0)
            t_pos = lax.broadcasted_iota(jnp.int32, (BK, BQ), 1)
            z = jnp.where(s_pos <= t_pos, z, NEG_BIG)
        m = m_ref[h]
        m_new = jnp.maximum(m, jnp.max(z, axis=0, keepdims=True))
        pexp = jnp.exp2(z - m_new)
        alpha = jnp.exp2(m - m_new)
        acc_ref[h] = alpha * acc_ref[h] + _dot(vt_ref[h], pexp.astype(BF16))
        m_ref[h] = m_new + tot_ref[j * N_HEADS + h]

    def tile(diagonal):
        for h in range(SCORE_SLOTS - 1):
            scores(h)
        for h in range(N_HEADS):
            if h + SCORE_SLOTS - 1 < N_HEADS:
                scores(h + SCORE_SLOTS - 1)
            softmax_pv(h, diagonal)
            if h % 2 == 0 and h // 2 < len(extra):
                extra[h // 2]()

    @pl.when(j < i)
    def _():
        tile(False)

    @pl.when(j == i)
    def _():
        tile(True)
        outs = []
        for h in range(N_HEADS):
            a = acc_ref[h]
            outs.append(a[0:HEAD_DIM, :] / a[HEAD_DIM:HEAD_DIM + 1, :])
        o_ref[...] = jnp.concatenate(outs, axis=0).T.astype(BF16)


def _page_copies(step, slot, pt_ref, kt_hbm, vt_hbm, lf_hbm, kbuf, vbuf, sbuf, sems, n_pages, n_chunks):
    ppc = n_pages // n_chunks
    b = step // n_chunks
    c = step % n_chunks
    copies = []
    for jj in range(ppc):
        pg = pt_ref[b * n_pages + n_pages - 1 - (c * ppc + jj)]
        copies.append(pltpu.make_async_copy(kt_hbm.at[pg], kbuf.at[slot, jj], sems.at[0, slot]))
        copies.append(pltpu.make_async_copy(vt_hbm.at[pg], vbuf.at[slot, jj], sems.at[1, slot]))
        copies.append(pltpu.make_async_copy(lf_hbm.at[pg], sbuf.at[slot, jj], sems.at[2, slot]))
    return copies


def _head_mask():
    aw = N_HEADS * HEAD_DIM
    lane_head = lax.broadcasted_iota(jnp.int32, (N_HEADS, aw), 1) // HEAD_DIM
    return lane_head == lax.broadcasted_iota(jnp.int32, (N_HEADS, aw), 0)


def _decode_init(q_ref, kn_ref, vn_ref, lft_ref, tri_ref, qbd_ref, m_ref, l_ref, acc_ref, carry_ref):
    t_new = q_ref.shape[0]
    aw = N_HEADS * HEAD_DIM
    n_rows = t_new * N_HEADS
    head_mask = _head_mask()
    q = q_ref[...] * (HEAD_DIM ** -0.5)
    qbd = jnp.concatenate(
        [jnp.where(head_mask, jnp.broadcast_to(q[t:t + 1, :], (N_HEADS, aw)), 0.0) for t in range(t_new)], axis=0)
    qbd_ref[...] = qbd.astype(BF16)
    carry_ref[...] = jnp.zeros(carry_ref.shape, F32)

    cnew = _dot3(_split3(lft_ref[...]), tri_ref[...])
    cnew = jnp.concatenate([cnew] * t_new, axis=0)
    t_of_row = lax.broadcasted_iota(jnp.int32, (n_rows, 1), 0) // N_HEADS
    kn = kn_ref[...]
    vn = vn_ref[...]
    zs = []
    for t2 in range(t_new):
        zc = jnp.sum(qbd * kn[t2:t2 + 1, :], axis=-1, keepdims=True)
        zc = zc - cnew[:, t2:t2 + 1]
        zs.append(jnp.where(t_of_row >= t2, zc, NEG_BIG))
    m = functools.reduce(jnp.maximum, zs)
    l = jnp.zeros((n_rows, 1), F32)
    acc = jnp.zeros((n_rows, aw), F32)
    for t2 in range(t_new):
        pe = jnp.exp(zs[t2] - m)
        l = l + pe
        acc = acc + pe * vn[t2:t2 + 1, :]
    m_ref[...] = m
    l_ref[...] = l
    acc_ref[...] = acc


def _decode_chunk_stages(slot, qbd_ref, m_ref, l_ref, acc_ref, carry_ref, kbuf, vbuf, sbuf, sfxm_ref, zd_ref, pd_ref,
                         al_ref):
    ppc = kbuf.shape[1]
    page = kbuf.shape[3]
    t_new = qbd_ref.shape[0] // N_HEADS

    def scores():
        carry = carry_ref[...]
        logf = sbuf[slot].reshape(ppc * N_HEADS, page)
        sfx_all = _dot3(_split3(logf), sfxm_ref[...])
        biases = []
        for jj in range(ppc):
            sfx = sfx_all[jj * N_HEADS:(jj + 1) * N_HEADS, :]
            biases.append(sfx[:, 0:page] + carry)
            carry = carry + sfx[:, LANES:LANES + page]
        carry_ref[...] = carry
        bias = jnp.concatenate(biases, axis=1)
        ktc = jnp.concatenate([kbuf[slot, jj].astype(BF16) for jj in range(ppc)], axis=1)
        zd_ref[...] = _dot(qbd_ref[...], ktc) + jnp.concatenate([bias] * t_new, axis=0)

    def softmax():
        z = zd_ref[...]
        m_prev = m_ref[...]
        m_new = jnp.maximum(m_prev, jnp.max(z, axis=-1, keepdims=True))
        pexp = jnp.exp(z - m_new)
        alpha = jnp.exp(m_prev - m_new)
        l_ref[...] = alpha * l_ref[...] + jnp.sum(pexp, axis=-1, keepdims=True)
        pd_ref[...] = pexp.astype(BF16)
        al_ref[...] = alpha
        m_ref[...] = m_new

    def values():
        vtc = jnp.concatenate([vbuf[slot, jj].astype(BF16) for jj in range(ppc)], axis=1)
        upd = lax.dot_general(pd_ref[...], vtc, (((1,), (1,)), ((), ())), preferred_element_type=F32)
        acc_ref[...] = al_ref[...] * acc_ref[...] + upd

    return [scores, softmax, values]


def _decode_final(o_ref, l_ref, acc_ref):
    t_new = o_ref.shape[0]
    head_mask = _head_mask()
    a = acc_ref[...] / l_ref[...]
    outs = []
    for t in range(t_new):
        blk = jnp.where(head_mask, a[t * N_HEADS:(t + 1) * N_HEADS, :], 0.0)
        outs.append(jnp.sum(blk, axis=0, keepdims=True))
    o_ref[...] = jnp.concatenate(outs, axis=0).astype(o_ref.dtype)


def _attention_kernel(qi_ref, kj_ref, tot_ref, pt_ref,
                      qt_ref, kaug_ref, vt_ref, q_ref, kn_ref, vn_ref, lft_ref, tri_ref, sfxm_ref, kt_hbm, vt_hbm, lf_hbm,
                      o_ref, os_ref,
                      fm_ref, facc_ref, z_ref,
                      qbd_ref, dm_ref, dl_ref, dacc_ref, carry_ref, kbuf, vbuf, sbuf, zd_ref, pd_ref, al_ref, sems,
                      *, n_pages, n_chunks, n_dec, n_pairs):
    p = pl.program_id(0)

    def copies(step):
        return _page_copies(step, step % 2, pt_ref, kt_hbm, vt_hbm, lf_hbm, kbuf, vbuf, sbuf, sems,
                            n_pages, n_chunks)

    @pl.when(p == 0)
    def _():
        for cp in copies(p):
            cp.start()

    @pl.when(p + 1 < n_dec)
    def _():
        for cp in copies(p + 1):
            cp.start()

    decoding = p < n_dec
    c = p % n_chunks

    @pl.when(decoding)
    def _():
        for cp in copies(p):
            cp.wait()

    @pl.when(decoding & (c == 0))
    def _():
        _decode_init(q_ref, kn_ref, vn_ref, lft_ref, tri_ref, qbd_ref, dm_ref, dl_ref, dacc_ref, carry_ref)

    stages = _decode_chunk_stages(p % 2, qbd_ref, dm_ref, dl_ref, dacc_ref, carry_ref, kbuf, vbuf, sbuf, sfxm_ref,
                                  zd_ref, pd_ref, al_ref)

    @pl.when(p < n_pairs)
    def _():
        _flash_kernel(qi_ref, kj_ref, tot_ref, qt_ref, kaug_ref, vt_ref, o_ref, fm_ref, facc_ref, z_ref, extra=stages)

    @pl.when((p >= n_pairs) & decoding)
    def _():
        for stage in stages:
            stage()

    @pl.when(decoding & (c == n_chunks - 1))
    def _():
        _decode_final(os_ref, dl_ref, dacc_ref)


def _attention(qt, kaug, vt, tot_flat, page_table, q_s, k_new, v_new, lft_pad, tri_new, sfx_mat, kt_pages, vt_pages,
               lf_pages):
    tpad = qt.shape[2]
    nq = tpad // BQ
    pairs = [(i, j) for i in range(nq) for j in range(i + 1)]
    aw = N_HEADS * HEAD_DIM
    db, t_new, _ = q_s.shape
    n_pages = page_table.shape[1]
    page = kt_pages.shape[2]
    ppc = min(PAGES_PER_STEP, n_pages)
    n_chunks = n_pages // ppc
    n_dec = db * n_chunks
    n_pairs = len(pairs)
    n_steps = max(n_pairs, n_dec)
    pairs = pairs + [pairs[-1]] * (n_steps - n_pairs)
    qi = jnp.asarray([p[0] for p in pairs], jnp.int32)
    kj = jnp.asarray([p[1] for p in pairs], jnp.int32)
    n_rows = t_new * N_HEADS

    seq3 = lambda p, qi, kj, tot, pt: (jnp.minimum(p // n_chunks, db - 1), 0, 0)
    grid_spec = pltpu.PrefetchScalarGridSpec(
        num_scalar_prefetch=4,
        grid=(n_steps,),
        in_specs=[pl.BlockSpec((N_HEADS, LANES, BQ), lambda p, qi, kj, tot, pt: (0, 0, qi[p])),
                  pl.BlockSpec((N_HEADS, BK, LANES), lambda p, qi, kj, tot, pt: (0, kj[p], 0)),
                  pl.BlockSpec((N_HEADS, VROWS, BK), lambda p, qi, kj, tot, pt: (0, 0, kj[p])),
                  pl.BlockSpec((None, t_new, aw), seq3),
                  pl.BlockSpec((None, t_new, aw), seq3),
                  pl.BlockSpec((None, t_new, aw), seq3),
                  pl.BlockSpec((None, N_HEADS, LANES), seq3),
                  pl.BlockSpec(tri_new.shape, lambda p, qi, kj, tot, pt: (0, 0)),
                  pl.BlockSpec(sfx_mat.shape, lambda p, qi, kj, tot, pt: (0, 0)),
                  pl.BlockSpec(memory_space=pl.ANY),
                  pl.BlockSpec(memory_space=pl.ANY),
                  pl.BlockSpec(memory_space=pl.ANY)],
        out_specs=[pl.BlockSpec((BQ, aw), lambda p, qi, kj, tot, pt: (qi[p], 0)),
                   pl.BlockSpec((None, t_new, aw), seq3)],
        scratch_shapes=[pltpu.VMEM((N_HEADS, 1, BQ), F32), pltpu.VMEM((N_HEADS, VROWS, BQ), F32),
                        pltpu.VMEM((SCORE_SLOTS, BK, BQ), F32),
                        pltpu.VMEM((n_rows, aw), BF16), pltpu.VMEM((n_rows, 1), F32), pltpu.VMEM((n_rows, 1), F32),
                        pltpu.VMEM((n_rows, aw), F32), pltpu.VMEM((N_HEADS, LANES), F32),
                        pltpu.VMEM((2, ppc, aw, page), F32), pltpu.VMEM((2, ppc, aw, page), F32),
                        pltpu.VMEM((2, ppc, N_HEADS, page), F32),
                        pltpu.VMEM((n_rows, ppc * page), F32), pltpu.VMEM((n_rows, ppc * page), BF16),
                        pltpu.VMEM((n_rows, 1), F32), pltpu.SemaphoreType.DMA((3, 2))],
    )
    kern = functools.partial(_attention_kernel, n_pages=n_pages, n_chunks=n_chunks, n_dec=n_dec, n_pairs=n_pairs)
    return pl.pallas_call(
        kern,
        grid_spec=grid_spec,
        out_shape=[jax.ShapeDtypeStruct((tpad, aw), BF16), jax.ShapeDtypeStruct((db, t_new, aw), BF16)],
        compiler_params=pltpu.CompilerParams(dimension_semantics=("arbitrary",), vmem_limit_bytes=VMEM_LIMIT),
        name="attention",
    )(qi, kj, tot_flat, page_table.reshape(-1), qt, kaug, vt, q_s, k_new, v_new, lft_pad, tri_new, sfx_mat,
      kt_pages, vt_pages, lf_pages)


def _mix_and_route(h, xn, a_bf, d_groups, w_ugg_ga_gb, poolw_ref, pscale_ref, wba_ref, wbp_ref, wout_ref,
                   g2_ref, wrt_ref, br_ref, h1_ref, xn2_ref, comb_ref, grow_ref):
    ga, gb = w_ugg_ga_gb
    mixed = [_dot(d_groups[g].astype(BF16), poolw_ref[g]) for g in range(len(POOL_WINDOWS))]
    pooled = jnp.concatenate(mixed, axis=-1) * pscale_ref[...]
    m = jax.nn.sigmoid(ga) * _dot(a_bf, wba_ref[...]) + jax.nn.sigmoid(gb) * _dot(pooled.astype(BF16), wbp_ref[...])
    h1 = h + _dot(m.astype(BF16), wout_ref[...])
    h1_ref[...] = h1
    xn2 = _rmsnorm(h1, g2_ref[...])
    xn2_bf = xn2.astype(BF16)
    xn2_ref[...] = xn2_bf

    lt = lax.dot_general(wrt_ref[...], xn2_bf, (((1,), (1,)), ((), ())), preferred_element_type=F32) + br_ref[...]
    g_rows = [lt[g:g + 1, :] for g in range(N_EXPERT_GROUPS)]
    gmax = functools.reduce(jnp.maximum, g_rows)
    gsum = functools.reduce(lambda a, b: a + b, [jnp.exp(r - gmax) for r in g_rows])
    pg_sel = 1.0 / gsum
    gsel = jnp.full(gmax.shape, N_EXPERT_GROUPS - 1, jnp.int32)
    for g in range(N_EXPERT_GROUPS - 2, -1, -1):
        gsel = jnp.where(g_rows[g] == gmax, g, gsel)
    e_rows = []
    for k in range(EXPERTS_PER_GROUP):
        r = lt[N_EXPERT_GROUPS + k:N_EXPERT_GROUPS + k + 1, :]
        for g in range(1, N_EXPERT_GROUPS):
            base = N_EXPERT_GROUPS + g * EXPERTS_PER_GROUP + k
            r = jnp.where(gsel == g, lt[base:base + 1, :], r)
        e_rows.append(r)
    v1 = functools.reduce(jnp.maximum, e_rows)
    i1 = jnp.full(gsel.shape, EXPERTS_PER_GROUP - 1, jnp.int32)
    for k in range(EXPERTS_PER_GROUP - 2, -1, -1):
        i1 = jnp.where(e_rows[k] == v1, k, i1)
    rest = [jnp.where(i1 == k, -jnp.inf, e_rows[k]) for k in range(EXPERTS_PER_GROUP)]
    v2 = functools.reduce(jnp.maximum, rest)
    i2 = jnp.full(gsel.shape, EXPERTS_PER_GROUP - 1, jnp.int32)
    for k in range(EXPERTS_PER_GROUP - 2, -1, -1):
        i2 = jnp.where((rest[k] == v2) & (i1 != k), k, i2)
    e2 = jnp.exp(v2 - v1)
    w1 = (1.0 / (1.0 + e2)) * pg_sel
    w2 = (e2 / (1.0 + e2)) * pg_sel
    id1 = gsel * EXPERTS_PER_GROUP + i1
    id2 = gsel * EXPERTS_PER_GROUP + i2
    ntok = lt.shape[1]
    erow = lax.broadcasted_iota(jnp.int32, (LANES, ntok), 0)
    comb_t = jnp.where(erow == id1, w1, 0.0) + jnp.where(erow == id2, w2, 0.0)
    comb_t = comb_t + jnp.where(erow == N_EXPERTS + gsel, 1.0, 0.0)
    comb_ref[...] = comb_t.T
    grow_ref[...] = jnp.where(lax.broadcasted_iota(jnp.int32, (8, ntok), 0) == gsel, 1.0, 0.0)


def _merge_prompt_kernel(head_ref, h_ref, a_ref, g1_ref, wugg_ref, poolw_ref, pscale_ref, wba_ref, wbp_ref, wout_ref,
                         g2_ref, wrt_ref, br_ref, h1_ref, xn2_ref, comb_ref, grow_ref, utail_ref, ext_ref, tail_ref):
    i = pl.program_id(0)

    @pl.when(i == pl.num_programs(0) - 1)
    def _():
        h1_ref[...] = jnp.zeros(h1_ref.shape, h1_ref.dtype)
        xn2_ref[...] = jnp.zeros(xn2_ref.shape, xn2_ref.dtype)
        comb_ref[...] = jnp.zeros(comb_ref.shape, comb_ref.dtype)
        grow_ref[...] = jnp.zeros(grow_ref.shape, grow_ref.dtype)

    @pl.when(i < pl.num_programs(0) - 1)
    def _():
        _merge_prompt_tile(i, _frame_tile(head_ref, h_ref), a_ref, g1_ref, wugg_ref, poolw_ref, pscale_ref, wba_ref, wbp_ref, wout_ref,
                           g2_ref, wrt_ref, br_ref, h1_ref, xn2_ref, comb_ref, grow_ref, utail_ref, ext_ref, tail_ref)


def _merge_prompt_tile(i, h, a_ref, g1_ref, wugg_ref, poolw_ref, pscale_ref, wba_ref, wbp_ref, wout_ref,
                       g2_ref, wrt_ref, br_ref, h1_ref, xn2_ref, comb_ref, grow_ref, utail_ref, ext_ref, tail_ref):
    pw = LANES * len(POOL_WINDOWS)
    hist_rows = 16

    @pl.when(i == 0)
    def _():
        tail_ref[...] = jnp.zeros(tail_ref.shape, F32)

    xn = _rmsnorm(h, g1_ref[...]).astype(BF16)
    ugg = _dot(xn, wugg_ref[...])
    d = h.shape[1]
    u = ugg[:, 0:pw]
    ga = ugg[:, pw:pw + d]
    gb = ugg[:, pw + d:pw + 2 * d]
    utail_ref[...] = u[TM - hist_rows:TM, :]

    ext_ref[0:hist_rows, :] = tail_ref[...]
    ext_ref[hist_rows:hist_rows + TM, :] = u
    tail_ref[...] = u[TM - hist_rows:TM, :]

    pos = lax.broadcasted_iota(jnp.int32, (TM, LANES), 0) + i * TM - FRONT
    d_groups = []
    for g, w in enumerate(POOL_WINDOWS):
        lanes = slice(g * LANES, (g + 1) * LANES)
        tok = ext_ref[hist_rows:hist_rows + TM, lanes]
        acc = tok
        for back in range(1, w):
            acc = acc + ext_ref[hist_rows - back:hist_rows - back + TM, lanes]
        cnt = jnp.clip(pos + 1, 1, w).astype(F32)
        d_groups.append(acc / cnt - tok)

    _mix_and_route(h, xn, a_ref[...], d_groups, (ga, gb), poolw_ref, pscale_ref, wba_ref, wbp_ref, wout_ref,
                   g2_ref, wrt_ref, br_ref, h1_ref, xn2_ref, comb_ref, grow_ref)


def _merge_sample_kernel(h_ref, a_ref, hist_ref, g1_ref, wugg_ref, poolw_ref, pscale_ref, wba_ref, wbp_ref, wout_ref,
                         g2_ref, wrt_ref, br_ref, h1_in, xn2_in, comb_in, grow_in, h1_ref, xn2_ref, comb_ref, grow_ref, u_ref,
                         *, pos0, db):
    pw = LANES * len(POOL_WINDOWS)
    h = h_ref[...]
    xn = _rmsnorm(h, g1_ref[...]).astype(BF16)
    ugg = _dot(xn, wugg_ref[...])
    d = h.shape[1]
    u = ugg[:, 0:pw]
    ga = ugg[:, pw:pw + d]
    gb = ugg[:, pw + d:pw + 2 * d]
    u_ref[...] = u
    t_new = h.shape[0] // db
    n_hist = hist_ref.shape[0]

    def ext(e, lanes):
        if e < n_hist:
            return hist_ref[e][:, lanes]
        return u[(e - n_hist) * db:(e - n_hist + 1) * db, lanes]

    d_groups = []
    for g, w in enumerate(POOL_WINDOWS):
        lanes = slice(g * LANES, (g + 1) * LANES)
        per_t = []
        for t in range(t_new):
            tok = ext(n_hist + t, lanes)
            acc = tok
            for back in range(1, w):
                acc = acc + ext(n_hist + t - back, lanes)
            cnt = float(min(pos0 + t + 1, w))
            per_t.append(acc / cnt - tok)
        d_groups.append(jnp.concatenate(per_t, axis=0))

    _mix_and_route(h, xn, a_ref[...], d_groups, (ga, gb), poolw_ref, pscale_ref, wba_ref, wbp_ref, wout_ref,
                   g2_ref, wrt_ref, br_ref, h1_ref, xn2_ref, comb_ref, grow_ref)


def _merge_weights_specs(ws):
    return [_full(w.shape) for w in ws]


def _merge_prompt(head, x, a_bf, ws, n_all):
    seq, d = x.shape
    nt = seq // TM + 1
    assert n_all == (nt + 1) * TM
    aw = a_bf.shape[1]
    pw = LANES * len(POOL_WINDOWS)
    x_row = lambda i: (jnp.clip(i - 1, 0, nt - 2), 0)
    a_row = lambda i: (jnp.minimum(i, nt - 1), 0)
    dst = lambda i: jnp.where(i == 0, nt - 1, jnp.where(i == nt, nt, i - 1))
    row = lambda i: (dst(i), 0)
    return pl.pallas_call(
        _merge_prompt_kernel,
        grid=(nt + 1,),
        in_specs=[_full(head.shape), pl.BlockSpec((TM, d), x_row), pl.BlockSpec((TM, aw), a_row)]
        + _merge_weights_specs(ws),
        out_specs=[pl.BlockSpec((TM, d), row), pl.BlockSpec((TM, d), row), pl.BlockSpec((TM, LANES), row),
                   pl.BlockSpec((8, TM), lambda i: (0, dst(i))), pl.BlockSpec((16, pw), lambda i: (0, 0))],
        out_shape=[jax.ShapeDtypeStruct((n_all, d), F32), jax.ShapeDtypeStruct((n_all, d), BF16),
                   jax.ShapeDtypeStruct((n_all, LANES), F32), jax.ShapeDtypeStruct((8, n_all), F32),
                   jax.ShapeDtypeStruct((16, pw), F32)],
        scratch_shapes=[pltpu.VMEM((TM + 16, pw), F32), pltpu.VMEM((16, pw), F32)],
        compiler_params=pltpu.CompilerParams(dimension_semantics=("arbitrary",), vmem_limit_bytes=VMEM_LIMIT),
        name="merge_prompt",
    )(head, x, a_bf, *ws)


def _merge_sample(xs, a_bf, hist_t, ws, all_arrays, pos0, db):
    n, d = xs.shape
    assert n == TM
    aw = a_bf.shape[1]
    pw = LANES * len(POOL_WINDOWS)
    last = all_arrays[0].shape[0] // TM - 1
    row = lambda i: (i, 0)
    tail = lambda i: (last, 0)
    kern = functools.partial(_merge_sample_kernel, pos0=pos0, db=db)
    n_in = 3 + len(ws)
    return pl.pallas_call(
        kern,
        grid=(1,),
        in_specs=[pl.BlockSpec((TM, d), row), pl.BlockSpec((TM, aw), row), _full(hist_t.shape)]
        + _merge_weights_specs(ws) + [pl.BlockSpec(memory_space=pl.ANY)] * 4,
        out_specs=[pl.BlockSpec((TM, d), tail), pl.BlockSpec((TM, d), tail), pl.BlockSpec((TM, LANES), tail),
                   pl.BlockSpec((8, TM), lambda i: (0, last)), pl.BlockSpec((TM, pw), row)],
        out_shape=[jax.ShapeDtypeStruct(a.shape, a.dtype) for a in all_arrays] + [jax.ShapeDtypeStruct((n, pw), F32)],
        input_output_aliases={n_in + k: k for k in range(4)},
        compiler_params=pltpu.CompilerParams(dimension_semantics=("arbitrary",), vmem_limit_bytes=VMEM_LIMIT),
        name="merge_sample",
    )(xs, a_bf, hist_t, *ws, *all_arrays)


MOE_TM = 1024
MOE_WIN = 320
MOE_ALIGN = 16
MOE_SORTED = MOE_TM + N_EXPERT_GROUPS * MOE_ALIGN
MOE_ROWS = MOE_SORTED + MOE_WIN
MOE_WINDOWS = -(-MOE_TM // MOE_WIN)


def _moe_kernel(x_ref, comb_ref, grow_ref, h1_ref, wg_ref, wu_ref, wd_ref, gf_ref, o_ref, otail_ref,
                xs_ref, cs_ref, acc_ref, pt_ref, meta_ref, *, n_main):
    i = pl.program_id(0)
    e = pl.program_id(1)
    tm = x_ref.shape[0]
    lane = lax.broadcasted_iota(jnp.int32, (tm, LANES), 1)

    @pl.when(e == 0)
    def _():
        onehot_rows = grow_ref[...]
        comb = comb_ref[...]
        onehot_cols = jnp.where((lane >= N_EXPERTS) & (lane < N_EXPERTS + N_EXPERT_GROUPS), comb, 0.0)
        r_idx = lax.broadcasted_iota(jnp.int32, (tm, tm), 0)
        c_idx = lax.broadcasted_iota(jnp.int32, (tm, tm), 1)
        upper = jnp.where(r_idx < c_idx, 1.0, 0.0).astype(BF16)
        lower = jnp.where(c_idx < r_idx, 1.0, 0.0).astype(BF16)
        rank_rows = _dot(onehot_rows.astype(BF16), upper)
        rank_cols = _dot(lower, onehot_cols.astype(BF16))
        pos_row = jnp.zeros((1, tm), F32)
        pos_col = jnp.zeros((tm, 1), F32)
        start = jnp.int32(0)
        for g in range(N_EXPERT_GROUPS):
            count = jnp.sum(onehot_rows[g:g + 1, :]).astype(jnp.int32)
            meta_ref[g] = start
            meta_ref[N_EXPERT_GROUPS + g] = count
            startf = start.astype(F32)
            pos_row = pos_row + onehot_rows[g:g + 1, :] * (rank_rows[g:g + 1, :] + startf)
            lg = N_EXPERTS + g
            pos_col = pos_col + onehot_cols[:, lg:lg + 1] * (rank_cols[:, lg:lg + 1] + startf)
            start = start + ((count + (MOE_ALIGN - 1)) // MOE_ALIGN) * MOE_ALIGN
        s_rows = lax.broadcasted_iota(jnp.int32, (MOE_SORTED, tm), 0)
        perm = jnp.where(s_rows == pos_row.astype(jnp.int32), 1.0, 0.0).astype(BF16)
        s_cols = lax.broadcasted_iota(jnp.int32, (tm, MOE_SORTED), 1)
        pt_ref[...] = jnp.where(s_cols == pos_col.astype(jnp.int32), 1.0, 0.0).astype(BF16)
        xs_ref[0:MOE_SORTED, :] = _dot(perm, x_ref[...]).astype(BF16)
        xs_ref[MOE_SORTED:MOE_ROWS, :] = jnp.zeros((MOE_ROWS - MOE_SORTED, x_ref.shape[1]), BF16)
        cs_ref[0:MOE_SORTED, :] = _dot3_left(perm, _split3(comb))
        cs_ref[MOE_SORTED:MOE_ROWS, :] = jnp.zeros((MOE_ROWS - MOE_SORTED, LANES), F32)
        acc_ref[...] = jnp.zeros(acc_ref.shape, F32)

    g = e // EXPERTS_PER_GROUP
    start = meta_ref[g]
    count = meta_ref[N_EXPERT_GROUPS + g]
    wlane = lax.broadcasted_iota(jnp.int32, (MOE_WIN, LANES), 1)
    for w in range(MOE_WINDOWS):
        @pl.when(w * MOE_WIN < count)
        def _():
            rows = pl.ds(pl.multiple_of(start + w * MOE_WIN, MOE_ALIGN), MOE_WIN)
            x = xs_ref[rows, :]
            hdn = jax.nn.silu(_dot(x, wg_ref[...])) * _dot(x, wu_ref[...])
            out = _dot(hdn.astype(BF16), wd_ref[...])
            col = jnp.sum(jnp.where(wlane == e, cs_ref[rows, :], 0.0), axis=-1, keepdims=True)
            acc_ref[rows, :] += col * out

    @pl.when(e == pl.num_programs(1) - 1)
    def _():
        y = acc_ref[0:MOE_SORTED, :]
        hi = y.astype(BF16)
        lo = (y - hi.astype(F32)).astype(BF16)
        back = pt_ref[...]
        res = _rmsnorm(h1_ref[...] + _dot(back, hi) + _dot(back, lo), gf_ref[...])

        @pl.when(i < n_main)
        def _():
            o_ref[...] = res

        @pl.when(i >= n_main)
        def _():
            otail_ref[...] = res


def _moe(xn2, comb, grow, h1, wg, wu, wd, gf, n_main_rows):
    n, d = xn2.shape
    nt = n // MOE_TM
    n_main = n_main_rows // MOE_TM
    assert n_main * MOE_TM == n_main_rows and nt == n_main + 1
    ne, _, de = wg.shape
    row = lambda i, e: (i, 0)
    return pl.pallas_call(
        functools.partial(_moe_kernel, n_main=n_main),
        grid=(nt, ne),
        in_specs=[pl.BlockSpec((MOE_TM, d), row), pl.BlockSpec((MOE_TM, LANES), row),
                  pl.BlockSpec((8, MOE_TM), lambda i, e: (0, i)), pl.BlockSpec((MOE_TM, d), row),
                  pl.BlockSpec((None, d, de), lambda i, e: (e, 0, 0)),
                  pl.BlockSpec((None, d, de), lambda i, e: (e, 0, 0)),
                  pl.BlockSpec((None, de, d), lambda i, e: (e, 0, 0)),
                  pl.BlockSpec(gf.shape, lambda i, e: (0, 0))],
        out_specs=[pl.BlockSpec((MOE_TM, d), lambda i, e: (jnp.minimum(i, n_main - 1), 0)),
                   pl.BlockSpec((MOE_TM, d), lambda i, e: (0, 0))],
        out_shape=[jax.ShapeDtypeStruct((n_main_rows, d), F32), jax.ShapeDtypeStruct((MOE_TM, d), F32)],
        scratch_shapes=[pltpu.VMEM((MOE_ROWS, d), BF16), pltpu.VMEM((MOE_ROWS, LANES), F32),
                        pltpu.VMEM((MOE_ROWS, d), F32), pltpu.VMEM((MOE_TM, MOE_SORTED), BF16),
                        pltpu.SMEM((2 * N_EXPERT_GROUPS,), jnp.int32)],
        compiler_params=pltpu.CompilerParams(dimension_semantics=("arbitrary", "arbitrary"),
                                             vmem_limit_bytes=VMEM_LIMIT),
        name="moe_grouped",
    )(xn2, comb, grow, h1, wg, wu, wd, gf)


def _placement_matrices():
    import numpy as np
    m = np.zeros((3, LANES, N_HEADS * LANES), np.float32)
    for h in range(N_HEADS):
        base = h * LANES + (HEAD_DIM if h % 2 == 0 else 0)
        for j in range(3):
            m[j, h, base + j] = 1.0
    return jnp.asarray(m, BF16)


def kernel(x_prompt, x_sample, cache_k, cache_v, cache_logf, state_pool, page_table, meta_tokens, norm1_g, w_in,
           b_forget, pool_w, pool_scale, w_branch_attn, w_branch_pool, w_out, norm2_g, w_router_group,
           b_router_group, w_router_expert, b_router_expert, w_gate, w_up, w_down, norm_f_g):
    import numpy as np
    depth = w_in.shape[0]
    assert depth == 1
    batch, seq, d = x_prompt.shape
    assert batch == 1
    db, t_new, _ = x_sample.shape
    assert db * t_new == TM
    aw = N_HEADS * HEAD_DIM
    pw = LANES * len(POOL_WINDOWS)
    seq_len = seq + N_META
    assert seq % MOE_TM == 0 and N_META <= 16
    tpad = seq + TM
    n_pool, page = cache_k.shape[1], cache_k.shape[2]
    n_pages = page_table.shape[1]
    past_len = n_pages * page
    assert page == LANES and t_new <= 8

    wl = w_in[0]
    w_qkvf = jnp.concatenate([wl[:, 0:3 * aw], jnp.pad(wl[:, 3 * aw:3 * aw + N_HEADS], ((0, 0), (0, LANES - N_HEADS)))],
                             axis=1).astype(BF16)
    w_ugg = wl[:, 3 * aw + N_HEADS:].astype(BF16)
    b_f = jnp.pad(b_forget[0], (0, LANES - N_HEADS)).reshape(1, LANES)
    g1 = norm1_g[0].reshape(1, d)
    g2 = norm2_g[0].reshape(1, d)
    gf = norm_f_g.reshape(1, d)
    n_r = N_EXPERT_GROUPS + N_EXPERTS
    w_rt = jnp.pad(jnp.concatenate([w_router_group[0], w_router_expert[0]], axis=1).T, ((0, 32 - n_r), (0, 0))).astype(BF16)
    b_r = jnp.pad(jnp.concatenate([b_router_group[0], b_router_expert[0]]), (0, 32 - n_r))
    b_r = jnp.broadcast_to(b_r[:, None], (32, TM))
    merge_ws = [g1, w_ugg, pool_w[0].astype(BF16), pool_scale[0].reshape(1, pw), w_branch_attn[0].astype(BF16),
                w_branch_pool[0].astype(BF16), w_out[0].astype(BF16), g2, w_rt, b_r]
    wg = w_gate[0].astype(BF16)
    wu = w_up[0].astype(BF16)
    wd = w_down[0].astype(BF16)

    tri_tm = jnp.asarray(np.tril(np.ones((TM, TM), np.float32)), BF16)
    place = _placement_matrices()

    head = jnp.concatenate([jnp.zeros((FRONT, d), x_prompt.dtype), meta_tokens.astype(x_prompt.dtype)], axis=0)
    k_f, v_f, lf_f, qt, kaug, vt, tot = _inproj_prompt(head, x_prompt[0], g1, w_qkvf, b_f, tri_tm, place)

    xs = jnp.transpose(x_sample, (1, 0, 2)).reshape(TM, d)
    q_s, k_s, v_s, lf_s, lft_s = _inproj_sample(xs, g1, w_qkvf, b_f)
    to_seq = lambda z: jnp.transpose(z.reshape(t_new, db, -1), (1, 0, 2))
    k_seq, v_seq = to_seq(k_s), to_seq(v_s)
    lft_pad = jnp.pad(jnp.transpose(lft_s.reshape(N_HEADS, t_new, db), (2, 0, 1)), ((0, 0), (0, 0), (0, LANES - t_new)))
    tri_new = jnp.asarray(np.triu(np.ones((LANES, LANES), np.float32)) * (np.arange(LANES) < t_new)[None, :], BF16)

    sfx_mat = jnp.asarray(np.concatenate([np.tril(np.ones((page, page), np.float32), -1),
                                          np.ones((page, page), np.float32)], axis=1), BF16)
    lf_pages = jnp.transpose(cache_logf[0], (0, 2, 1))

    kt_pages = jnp.transpose(cache_k[0], (0, 2, 3, 1)).reshape(n_pool, aw, page)
    vt_pages = jnp.transpose(cache_v[0], (0, 2, 3, 1)).reshape(n_pool, aw, page)
    a_p, a_s = _attention(qt, kaug, vt, tot[:, 0, 0:N_HEADS].reshape(-1), page_table, to_seq(q_s), k_seq, v_seq,
                          lft_pad, tri_new, sfx_mat, kt_pages, vt_pages, lf_pages)
    n_all = tpad + TM
    h1_a, xn2_a, comb_a, grow_a, utail = _merge_prompt(head, x_prompt[0], a_p, merge_ws, n_all)
    a_s_t = jnp.transpose(a_s, (1, 0, 2)).reshape(TM, aw)
    hist_t = jnp.transpose(state_pool[0], (1, 0, 2))
    h1_a, xn2_a, comb_a, grow_a, u_s = _merge_sample(xs, a_s_t, hist_t, merge_ws, (h1_a, xn2_a, comb_a, grow_a),
                                                     past_len, db)
    y_x, y_tail = _moe(xn2_a, comb_a, grow_a, h1_a, wg, wu, wd, gf, seq)

    y_prompt = y_x.reshape(1, seq, d)
    y_sample = to_seq(y_tail[TM:])
    k_prompt = k_f[FRONT:].reshape(1, 1, seq_len, N_HEADS, HEAD_DIM)
    v_prompt = v_f[FRONT:].reshape(1, 1, seq_len, N_HEADS, HEAD_DIM)
    logf_prompt = lf_f[FRONT:].reshape(1, 1, seq_len, N_HEADS)
    pool_prompt = utail[16 - POOL_HIST:16].reshape(1, 1, POOL_HIST, pw)
    k_sample = k_seq.reshape(1, db, t_new, N_HEADS, HEAD_DIM)
    v_sample = v_seq.reshape(1, db, t_new, N_HEADS, HEAD_DIM)
    logf_sample = to_seq(lf_s).reshape(1, db, t_new, N_HEADS)
    pool_sample = jnp.concatenate([state_pool[0].astype(F32), to_seq(u_s)], axis=1)[:, -POOL_HIST:].reshape(
        1, db, POOL_HIST, pw)
    return (y_prompt, y_sample, k_prompt, v_prompt, logf_prompt, pool_prompt,
            k_sample, v_sample, logf_sample, pool_sample)
```

```python
import functools

import jax
import jax.numpy as jnp
from jax import lax
from jax.experimental import pallas as pl
from jax.experimental.pallas import tpu as pltpu

F32 = jnp.float32
BF16 = jnp.bfloat16

N_HEADS = 8
HEAD_DIM = 64
N_META = 16
POOL_WINDOWS = (2, 4, 8, 16)
POOL_HIST = 15
N_EXPERT_GROUPS = 4
EXPERTS_PER_GROUP = 4
N_EXPERTS = 16
RMS_EPS = 1e-6

LANES = 128
TM = 512
BQ = 512
BK = 512
PAGES_PER_STEP = 16
SCORE_SLOTS = 3
VROWS = HEAD_DIM + 16
NEG_BIG = -1e30
FRONT = TM - N_META
LOG2E = 1.4426950408889634
VMEM_LIMIT = 52 * 1024 * 1024


def _rmsnorm(x, g):
    inv = lax.rsqrt(jnp.mean(x * x, axis=-1, keepdims=True) + RMS_EPS)
    return (x * inv) * g


def _log_sigmoid(x):
    return jnp.minimum(x, 0.0) - jnp.log1p(jnp.exp(-jnp.abs(x)))


def _split3(x):
    hi = x.astype(BF16)
    r1 = x - hi.astype(F32)
    mid = r1.astype(BF16)
    lo = (r1 - mid.astype(F32)).astype(BF16)
    return hi, mid, lo


def _dot(a, b):
    return jnp.dot(a, b, preferred_element_type=F32)


def _dot3(parts, m):
    return _dot(parts[0], m) + _dot(parts[1], m) + _dot(parts[2], m)


def _dot3_left(m, parts):
    return _dot(m, parts[0]) + _dot(m, parts[1]) + _dot(m, parts[2])


def _inproj_common(x, g_ref, w_ref, bf_ref):
    aw = N_HEADS * HEAD_DIM
    xn = _rmsnorm(x, g_ref[...]).astype(BF16)
    proj = _dot(xn, w_ref[...])
    q = proj[:, 0:aw]
    k = proj[:, aw:2 * aw]
    v = proj[:, 2 * aw:3 * aw]
    lf = _log_sigmoid(proj[:, 3 * aw:3 * aw + LANES] + bf_ref[...])
    return q, k, v, lf


def _frame_tile(head_ref, x_ref):
    return jnp.where(pl.program_id(0) == 0, head_ref[...], x_ref[...])


def _kv_row_copies(i, k_hbm, v_hbm, kbuf, vbuf, sems):
    first = [pltpu.make_async_copy(buf.at[pl.ds(FRONT, N_META)], hbm.at[pl.ds(0, N_META)], sems.at[n])
             for n, (buf, hbm) in enumerate(((kbuf, k_hbm), (vbuf, v_hbm)))]
    start = pl.multiple_of(N_META + (i - 1) * TM, N_META)
    later = [pltpu.make_async_copy(buf, hbm.at[pl.ds(start, TM)], sems.at[n])
             for n, (buf, hbm) in enumerate(((kbuf, k_hbm), (vbuf, v_hbm)))]
    return first, later


def _inproj_prompt_kernel(head_ref, x_ref, g_ref, w_ref, bf_ref, tri_ref, place_ref,
                          k_hbm, v_hbm, lf_ref, qt_ref, kaug_ref, vt_ref, tot_ref, kbuf, vbuf, sems):
    i = pl.program_id(0)
    q, k, v, lf = _inproj_common(_frame_tile(head_ref, x_ref), g_ref, w_ref, bf_ref)
    kbuf[...] = k
    vbuf[...] = v
    first, later = _kv_row_copies(i, k_hbm, v_hbm, kbuf, vbuf, sems)

    @pl.when(i == 0)
    def _():
        for cp in first:
            cp.start()

    @pl.when(i > 0)
    def _():
        for cp in later:
            cp.start()

    lf_ref[...] = lf[:, 0:N_HEADS]

    real = lax.broadcasted_iota(jnp.int32, (TM, LANES), 0) + pl.program_id(0) * TM >= FRONT
    cs = _dot3_left(tri_ref[...], _split3(jnp.where(real, lf, 0.0))) * LOG2E
    tot_ref[0] = cs[TM - 1:TM, :]

    c3 = _split3(jnp.where(real, cs, -NEG_BIG))
    aug = _dot(c3[0], place_ref[0]) + _dot(c3[1], place_ref[1]) + _dot(c3[2], place_ref[2])

    lane = lax.broadcasted_iota(jnp.int32, (TM, LANES), 1)
    low = lane < HEAD_DIM
    qt = (q * (HEAD_DIM ** -0.5 * LOG2E)).T
    vt = v.T
    row = lax.broadcasted_iota(jnp.int32, (HEAD_DIM, TM), 0)
    minus_ones = jnp.where(row < 3, -1.0, 0.0).astype(BF16)
    ones_row = jnp.where(lax.broadcasted_iota(jnp.int32, (VROWS - HEAD_DIM, TM), 0) == 0, 1.0, 0.0).astype(BF16)
    for h in range(N_HEADS):
        kp = k[:, (h // 2) * LANES:(h // 2 + 1) * LANES]
        own = low if h % 2 == 0 else jnp.logical_not(low)
        kaug_ref[h] = (jnp.where(own, kp, 0.0) + aug[:, h * LANES:(h + 1) * LANES]).astype(BF16)
        qh = qt[h * HEAD_DIM:(h + 1) * HEAD_DIM, :].astype(BF16)
        if h % 2 == 0:
            qt_ref[h, 0:HEAD_DIM, :] = qh
            qt_ref[h, HEAD_DIM:2 * HEAD_DIM, :] = minus_ones
        else:
            qt_ref[h, 0:HEAD_DIM, :] = minus_ones
            qt_ref[h, HEAD_DIM:2 * HEAD_DIM, :] = qh
        vt_ref[h, 0:HEAD_DIM, :] = vt[h * HEAD_DIM:(h + 1) * HEAD_DIM, :].astype(BF16)
        vt_ref[h, HEAD_DIM:VROWS, :] = ones_row

    @pl.when(i == 0)
    def _():
        for cp in first:
            cp.wait()

    @pl.when(i > 0)
    def _():
        for cp in later:
            cp.wait()


def _inproj_sample_kernel(x_ref, g_ref, w_ref, bf_ref, q_ref, k_ref, v_ref, lf_ref, lft_ref):
    q, k, v, lf = _inproj_common(x_ref[...], g_ref, w_ref, bf_ref)
    q_ref[...] = q
    k_ref[...] = k
    v_ref[...] = v
    lf_ref[...] = lf[:, 0:N_HEADS]
    lft_ref[...] = lf.T[0:N_HEADS, :]


def _full(shape):
    n = len(shape)
    return pl.BlockSpec(shape, lambda *_: (0,) * n)


def _inproj_prompt(head, x, g1, w_qkvf, b_f, tri, place):
    seq, d = x.shape
    tpad = seq + TM
    nt = tpad // TM
    aw = N_HEADS * HEAD_DIM
    row = lambda i: (i, 0)
    x_row = lambda i: (jnp.maximum(i - 1, 0), 0)
    return pl.pallas_call(
        _inproj_prompt_kernel,
        grid=(nt,),
        in_specs=[_full(head.shape), pl.BlockSpec((TM, d), x_row), _full(g1.shape), _full(w_qkvf.shape),
                  _full(b_f.shape), _full(tri.shape), _full(place.shape)],
        out_specs=[pl.BlockSpec(memory_space=pl.ANY), pl.BlockSpec(memory_space=pl.ANY),
                   pl.BlockSpec((TM, N_HEADS), row),
                   pl.BlockSpec((N_HEADS, LANES, TM), lambda i: (0, 0, i)),
                   pl.BlockSpec((N_HEADS, TM, LANES), lambda i: (0, i, 0)),
                   pl.BlockSpec((N_HEADS, VROWS, TM), lambda i: (0, 0, i)),
                   pl.BlockSpec((1, 1, LANES), lambda i: (i, 0, 0))],
        out_shape=[jax.ShapeDtypeStruct((seq + N_META, aw), F32), jax.ShapeDtypeStruct((seq + N_META, aw), F32),
                   jax.ShapeDtypeStruct((tpad, N_HEADS), F32),
                   jax.ShapeDtypeStruct((N_HEADS, LANES, tpad), BF16),
                   jax.ShapeDtypeStruct((N_HEADS, tpad, LANES), BF16),
                   jax.ShapeDtypeStruct((N_HEADS, VROWS, tpad), BF16),
                   jax.ShapeDtypeStruct((nt, 1, LANES), F32)],
        scratch_shapes=[pltpu.VMEM((TM, aw), F32), pltpu.VMEM((TM, aw), F32), pltpu.SemaphoreType.DMA((2,))],
        compiler_params=pltpu.CompilerParams(dimension_semantics=("arbitrary",), vmem_limit_bytes=VMEM_LIMIT),
        name="inproj_prompt",
    )(head, x, g1, w_qkvf, b_f, tri, place)


def _inproj_sample(xs, g1, w_qkvf, b_f):
    n, d = xs.shape
    nt = n // TM
    aw = N_HEADS * HEAD_DIM
    row = lambda i: (i, 0)
    return pl.pallas_call(
        _inproj_sample_kernel,
        grid=(nt,),
        in_specs=[pl.BlockSpec((TM, d), row), _full(g1.shape), _full(w_qkvf.shape), _full(b_f.shape)],
        out_specs=[pl.BlockSpec((TM, aw), row)] * 3 + [pl.BlockSpec((TM, N_HEADS), row),
                                                        pl.BlockSpec((N_HEADS, TM), lambda i: (0, i))],
        out_shape=[jax.ShapeDtypeStruct((n, aw), F32)] * 3 + [jax.ShapeDtypeStruct((n, N_HEADS), F32),
                                                               jax.ShapeDtypeStruct((N_HEADS, n), F32)],
        compiler_params=pltpu.CompilerParams(dimension_semantics=("arbitrary",), vmem_limit_bytes=VMEM_LIMIT),
        name="inproj_sample",
    )(xs, g1, w_qkvf, b_f)


def _flash_kernel(qi_ref, kj_ref, tot_ref, qt_ref, kaug_ref, vt_ref, o_ref, m_ref, acc_ref, z_ref, extra=()):
    p = pl.program_id(0)
    i = qi_ref[p]
    j = kj_ref[p]

    @pl.when(j == 0)
    def _():
        m_ref[...] = jnp.full(m_ref.shape, NEG_BIG, F32)
        acc_ref[...] = jnp.zeros(acc_ref.shape, F32)

    def scores(h):
        z_ref[h % SCORE_SLOTS] = _dot(kaug_ref[h], qt_ref[h])

    def softmax_pv(h, diagonal):
        z = z_ref[h % SCORE_SLOTS]
        if diagonal:
            s_pos = lax.broadcasted_iota(jnp.int32, (BK, BQ), 0)
            t_pos = lax.broadcasted_iota(jnp.int32, (BK, BQ), 1)
            z = jnp.where(s_pos <= t_pos, z, NEG_BIG)
        m = m_ref[h]
        m_new = jnp.maximum(m, jnp.max(z, axis=0, keepdims=True))
        pexp = jnp.exp2(z - m_new)
        alpha = jnp.exp2(m - m_new)
        acc_ref[h] = alpha * acc_ref[h] + _dot(vt_ref[h], pexp.astype(BF16))
        m_ref[h] = m_new + tot_ref[j * N_HEADS + h]

    def tile(diagonal):
        for h in range(SCORE_SLOTS - 1):
            scores(h)
        for h in range(N_HEADS):
            if h + SCORE_SLOTS - 1 < N_HEADS:
                scores(h + SCORE_SLOTS - 1)
            softmax_pv(h, diagonal)
            if h % 2 == 0 and h // 2 < len(extra):
                extra[h // 2]()

    @pl.when(j < i)
    def _():
        tile(False)

    @pl.when(j == i)
    def _():
        tile(True)
        outs = []
        for h in range(N_HEADS):
            a = acc_ref[h]
            outs.append(a[0:HEAD_DIM, :] / a[HEAD_DIM:HEAD_DIM + 1, :])
        o_ref[...] = jnp.concatenate(outs, axis=0).T.astype(BF16)


def _page_copies(step, slot, pt_ref, kt_hbm, vt_hbm, lf_hbm, kbuf, vbuf, sbuf, sems, n_pages, n_chunks):
    ppc = n_pages // n_chunks
    b = step // n_chunks
    c = step % n_chunks
    copies = []
    for jj in range(ppc):
        pg = pt_ref[b * n_pages + n_pages - 1 - (c * ppc + jj)]
        copies.append(pltpu.make_async_copy(kt_hbm.at[pg], kbuf.at[slot, jj], sems.at[0, slot]))
        copies.append(pltpu.make_async_copy(vt_hbm.at[pg], vbuf.at[slot, jj], sems.at[1, slot]))
        copies.append(pltpu.make_async_copy(lf_hbm.at[pg], sbuf.at[slot, jj], sems.at[2, slot]))
    return copies


def _head_mask():
    aw = N_HEADS * HEAD_DIM
    lane_head = lax.broadcasted_iota(jnp.int32, (N_HEADS, aw), 1) // HEAD_DIM
    return lane_head == lax.broadcasted_iota(jnp.int32, (N_HEADS, aw), 0)


def _decode_init(q_ref, kn_ref, vn_ref, lft_ref, tri_ref, qbd_ref, m_ref, l_ref, acc_ref, carry_ref):
    t_new = q_ref.shape[0]
    aw = N_HEADS * HEAD_DIM
    n_rows = t_new * N_HEADS
    head_mask = _head_mask()
    q = q_ref[...] * (HEAD_DIM ** -0.5)
    qbd = jnp.concatenate(
        [jnp.where(head_mask, jnp.broadcast_to(q[t:t + 1, :], (N_HEADS, aw)), 0.0) for t in range(t_new)], axis=0)
    qbd_ref[...] = qbd.astype(BF16)
    carry_ref[...] = jnp.zeros(carry_ref.shape, F32)

    cnew = _dot3(_split3(lft_ref[...]), tri_ref[...])
    cnew = jnp.concatenate([cnew] * t_new, axis=0)
    t_of_row = lax.broadcasted_iota(jnp.int32, (n_rows, 1), 0) // N_HEADS
    kn = kn_ref[...]
    vn = vn_ref[...]
    zs = []
    for t2 in range(t_new):
        zc = jnp.sum(qbd * kn[t2:t2 + 1, :], axis=-1, keepdims=True)
        zc = zc - cnew[:, t2:t2 + 1]
        zs.append(jnp.where(t_of_row >= t2, zc, NEG_BIG))
    m = functools.reduce(jnp.maximum, zs)
    l = jnp.zeros((n_rows, 1), F32)
    acc = jnp.zeros((n_rows, aw), F32)
    for t2 in range(t_new):
        pe = jnp.exp(zs[t2] - m)
        l = l + pe
        acc = acc + pe * vn[t2:t2 + 1, :]
    m_ref[...] = m
    l_ref[...] = l
    acc_ref[...] = acc


def _decode_chunk_stages(slot, qbd_ref, m_ref, l_ref, acc_ref, carry_ref, kbuf, vbuf, sbuf, sfxm_ref, zd_ref, pd_ref,
                         al_ref):
    ppc = kbuf.shape[1]
    page = kbuf.shape[3]
    t_new = qbd_ref.shape[0] // N_HEADS

    def scores():
        carry = carry_ref[...]
        logf = sbuf[slot].reshape(ppc * N_HEADS, page)
        sfx_all = _dot3(_split3(logf), sfxm_ref[...])
        biases = []
        for jj in range(ppc):
            sfx = sfx_all[jj * N_HEADS:(jj + 1) * N_HEADS, :]
            biases.append(sfx[:, 0:page] + carry)
            carry = carry + sfx[:, LANES:LANES + page]
        carry_ref[...] = carry
        bias = jnp.concatenate(biases, axis=1)
        ktc = jnp.concatenate([kbuf[slot, jj].astype(BF16) for jj in range(ppc)], axis=1)
        zd_ref[...] = _dot(qbd_ref[...], ktc) + jnp.concatenate([bias] * t_new, axis=0)

    def softmax():
        z = zd_ref[...]
        m_prev = m_ref[...]
        m_new = jnp.maximum(m_prev, jnp.max(z, axis=-1, keepdims=True))
        pexp = jnp.exp(z - m_new)
        alpha = jnp.exp(m_prev - m_new)
        l_ref[...] = alpha * l_ref[...] + jnp.sum(pexp, axis=-1, keepdims=True)
        pd_ref[...] = pexp.astype(BF16)
        al_ref[...] = alpha
        m_ref[...] = m_new

    def values():
        vtc = jnp.concatenate([vbuf[slot, jj].astype(BF16) for jj in range(ppc)], axis=1)
        upd = lax.dot_general(pd_ref[...], vtc, (((1,), (1,)), ((), ())), preferred_element_type=F32)
        acc_ref[...] = al_ref[...] * acc_ref[...] + upd

    return [scores, softmax, values]


def _decode_final(o_ref, l_ref, acc_ref):
    t_new = o_ref.shape[0]
    head_mask = _head_mask()
    a = acc_ref[...] / l_ref[...]
    outs = []
    for t in range(t_new):
        blk = jnp.where(head_mask, a[t * N_HEADS:(t + 1) * N_HEADS, :], 0.0)
        outs.append(jnp.sum(blk, axis=0, keepdims=True))
    o_ref[...] = jnp.concatenate(outs, axis=0).astype(o_ref.dtype)


def _attention_kernel(qi_ref, kj_ref, tot_ref, pt_ref,
                      qt_ref, kaug_ref, vt_ref, q_ref, kn_ref, vn_ref, lft_ref, tri_ref, sfxm_ref, kt_hbm, vt_hbm, lf_hbm,
                      o_ref, os_ref,
                      fm_ref, facc_ref, z_ref,
                      qbd_ref, dm_ref, dl_ref, dacc_ref, carry_ref, kbuf, vbuf, sbuf, zd_ref, pd_ref, al_ref, sems,
                      *, n_pages, n_chunks, n_dec, n_pairs):
    p = pl.program_id(0)

    def copies(step):
        return _page_copies(step, step % 2, pt_ref, kt_hbm, vt_hbm, lf_hbm, kbuf, vbuf, sbuf, sems,
                            n_pages, n_chunks)

    @pl.when(p == 0)
    def _():
        for cp in copies(p):
            cp.start()

    @pl.when(p + 1 < n_dec)
    def _():
        for cp in copies(p + 1):
            cp.start()

    decoding = p < n_dec
    c = p % n_chunks

    @pl.when(decoding)
    def _():
        for cp in copies(p):
            cp.wait()

    @pl.when(decoding & (c == 0))
    def _():
        _decode_init(q_ref, kn_ref, vn_ref, lft_ref, tri_ref, qbd_ref, dm_ref, dl_ref, dacc_ref, carry_ref)

    stages = _decode_chunk_stages(p % 2, qbd_ref, dm_ref, dl_ref, dacc_ref, carry_ref, kbuf, vbuf, sbuf, sfxm_ref,
                                  zd_ref, pd_ref, al_ref)

    @pl.when(p < n_pairs)
    def _():
        _flash_kernel(qi_ref, kj_ref, tot_ref, qt_ref, kaug_ref, vt_ref, o_ref, fm_ref, facc_ref, z_ref, extra=stages)

    @pl.when((p >= n_pairs) & decoding)
    def _():
        for stage in stages:
            stage()

    @pl.when(decoding & (c == n_chunks - 1))
    def _():
        _decode_final(os_ref, dl_ref, dacc_ref)


def _attention(qt, kaug, vt, tot_flat, page_table, q_s, k_new, v_new, lft_pad, tri_new, sfx_mat, kt_pages, vt_pages,
               lf_pages):
    tpad = qt.shape[2]
    nq = tpad // BQ
    pairs = [(i, j) for i in range(nq) for j in range(i + 1)]
    aw = N_HEADS * HEAD_DIM
    db, t_new, _ = q_s.shape
    n_pages = page_table.shape[1]
    page = kt_pages.shape[2]
    ppc = min(PAGES_PER_STEP, n_pages)
    n_chunks = n_pages // ppc
    n_dec = db * n_chunks
    n_pairs = len(pairs)
    n_steps = max(n_pairs, n_dec)
    pairs = pairs + [pairs[-1]] * (n_steps - n_pairs)
    qi = jnp.asarray([p[0] for p in pairs], jnp.int32)
    kj = jnp.asarray([p[1] for p in pairs], jnp.int32)
    n_rows = t_new * N_HEADS

    seq3 = lambda p, qi, kj, tot, pt: (jnp.minimum(p // n_chunks, db - 1), 0, 0)
    grid_spec = pltpu.PrefetchScalarGridSpec(
        num_scalar_prefetch=4,
        grid=(n_steps,),
        in_specs=[pl.BlockSpec((N_HEADS, LANES, BQ), lambda p, qi, kj, tot, pt: (0, 0, qi[p])),
                  pl.BlockSpec((N_HEADS, BK, LANES), lambda p, qi, kj, tot, pt: (0, kj[p], 0)),
                  pl.BlockSpec((N_HEADS, VROWS, BK), lambda p, qi, kj, tot, pt: (0, 0, kj[p])),
                  pl.BlockSpec((None, t_new, aw), seq3),
                  pl.BlockSpec((None, t_new, aw), seq3),
                  pl.BlockSpec((None, t_new, aw), seq3),
                  pl.BlockSpec((None, N_HEADS, LANES), seq3),
                  pl.BlockSpec(tri_new.shape, lambda p, qi, kj, tot, pt: (0, 0)),
                  pl.BlockSpec(sfx_mat.shape, lambda p, qi, kj, tot, pt: (0, 0)),
                  pl.BlockSpec(memory_space=pl.ANY),
                  pl.BlockSpec(memory_space=pl.ANY),
                  pl.BlockSpec(memory_space=pl.ANY)],
        out_specs=[pl.BlockSpec((BQ, aw), lambda p, qi, kj, tot, pt: (qi[p], 0)),
                   pl.BlockSpec((None, t_new, aw), seq3)],
        scratch_shapes=[pltpu.VMEM((N_HEADS, 1, BQ), F32), pltpu.VMEM((N_HEADS, VROWS, BQ), F32),
                        pltpu.VMEM((SCORE_SLOTS, BK, BQ), F32),
                        pltpu.VMEM((n_rows, aw), BF16), pltpu.VMEM((n_rows, 1), F32), pltpu.VMEM((n_rows, 1), F32),
                        pltpu.VMEM((n_rows, aw), F32), pltpu.VMEM((N_HEADS, LANES), F32),
                        pltpu.VMEM((2, ppc, aw, page), F32), pltpu.VMEM((2, ppc, aw, page), F32),
                        pltpu.VMEM((2, ppc, N_HEADS, page), F32),
                        pltpu.VMEM((n_rows, ppc * page), F32), pltpu.VMEM((n_rows, ppc * page), BF16),
                        pltpu.VMEM((n_rows, 1), F32), pltpu.SemaphoreType.DMA((3, 2))],
    )
    kern = functools.partial(_attention_kernel, n_pages=n_pages, n_chunks=n_chunks, n_dec=n_dec, n_pairs=n_pairs)
    return pl.pallas_call(
        kern,
        grid_spec=grid_spec,
        out_shape=[jax.ShapeDtypeStruct((tpad, aw), BF16), jax.ShapeDtypeStruct((db, t_new, aw), BF16)],
        compiler_params=pltpu.CompilerParams(dimension_semantics=("arbitrary",), vmem_limit_bytes=VMEM_LIMIT),
        name="attention",
    )(qi, kj, tot_flat, page_table.reshape(-1), qt, kaug, vt, q_s, k_new, v_new, lft_pad, tri_new, sfx_mat,
      kt_pages, vt_pages, lf_pages)


def _mix_and_route(h, xn, a_bf, d_groups, w_ugg_ga_gb, poolw_ref, pscale_ref, wba_ref, wbp_ref, wout_ref,
                   g2_ref, wrt_ref, br_ref, h1_ref, xn2_ref, comb_ref, grow_ref):
    ga, gb = w_ugg_ga_gb
    mixed = [_dot(d_groups[g].astype(BF16), poolw_ref[g]) for g in range(len(POOL_WINDOWS))]
    pooled = jnp.concatenate(mixed, axis=-1) * pscale_ref[...]
    m = jax.nn.sigmoid(ga) * _dot(a_bf, wba_ref[...]) + jax.nn.sigmoid(gb) * _dot(pooled.astype(BF16), wbp_ref[...])
    h1 = h + _dot(m.astype(BF16), wout_ref[...])
    h1_ref[...] = h1
    xn2 = _rmsnorm(h1, g2_ref[...])
    xn2_bf = xn2.astype(BF16)
    xn2_ref[...] = xn2_bf

    lt = lax.dot_general(wrt_ref[...], xn2_bf, (((1,), (1,)), ((), ())), preferred_element_type=F32) + br_ref[...]
    g_rows = [lt[g:g + 1, :] for g in range(N_EXPERT_GROUPS)]
    gmax = functools.reduce(jnp.maximum, g_rows)
    gsum = functools.reduce(lambda a, b: a + b, [jnp.exp(r - gmax) for r in g_rows])
    pg_sel = 1.0 / gsum
    gsel = jnp.full(gmax.shape, N_EXPERT_GROUPS - 1, jnp.int32)
    for g in range(N_EXPERT_GROUPS - 2, -1, -1):
        gsel = jnp.where(g_rows[g] == gmax, g, gsel)
    e_rows = []
    for k in range(EXPERTS_PER_GROUP):
        r = lt[N_EXPERT_GROUPS + k:N_EXPERT_GROUPS + k + 1, :]
        for g in range(1, N_EXPERT_GROUPS):
            base = N_EXPERT_GROUPS + g * EXPERTS_PER_GROUP + k
            r = jnp.where(gsel == g, lt[base:base + 1, :], r)
        e_rows.append(r)
    v1 = functools.reduce(jnp.maximum, e_rows)
    i1 = jnp.full(gsel.shape, EXPERTS_PER_GROUP - 1, jnp.int32)
    for k in range(EXPERTS_PER_GROUP - 2, -1, -1):
        i1 = jnp.where(e_rows[k] == v1, k, i1)
    rest = [jnp.where(i1 == k, -jnp.inf, e_rows[k]) for k in range(EXPERTS_PER_GROUP)]
    v2 = functools.reduce(jnp.maximum, rest)
    i2 = jnp.full(gsel.shape, EXPERTS_PER_GROUP - 1, jnp.int32)
    for k in range(EXPERTS_PER_GROUP - 2, -1, -1):
        i2 = jnp.where((rest[k] == v2) & (i1 != k), k, i2)
    e2 = jnp.exp(v2 - v1)
    w1 = (1.0 / (1.0 + e2)) * pg_sel
    w2 = (e2 / (1.0 + e2)) * pg_sel
    id1 = gsel * EXPERTS_PER_GROUP + i1
    id2 = gsel * EXPERTS_PER_GROUP + i2
    ntok = lt.shape[1]
    erow = lax.broadcasted_iota(jnp.int32, (LANES, ntok), 0)
    comb_t = jnp.where(erow == id1, w1, 0.0) + jnp.where(erow == id2, w2, 0.0)
    comb_t = comb_t + jnp.where(erow == N_EXPERTS + gsel, 1.0, 0.0)
    comb_ref[...] = comb_t.T
    grow_ref[...] = jnp.where(lax.broadcasted_iota(jnp.int32, (8, ntok), 0) == gsel, 1.0, 0.0)


def _merge_prompt_kernel(head_ref, h_ref, a_ref, g1_ref, wugg_ref, poolw_ref, pscale_ref, wba_ref, wbp_ref, wout_ref,
                         g2_ref, wrt_ref, br_ref, h1_ref, xn2_ref, comb_ref, grow_ref, utail_ref, ext_ref, tail_ref):
    i = pl.program_id(0)

    @pl.when(i == pl.num_programs(0) - 1)
    def _():
        h1_ref[...] = jnp.zeros(h1_ref.shape, h1_ref.dtype)
        xn2_ref[...] = jnp.zeros(xn2_ref.shape, xn2_ref.dtype)
        comb_ref[...] = jnp.zeros(comb_ref.shape, comb_ref.dtype)
        grow_ref[...] = jnp.zeros(grow_ref.shape, grow_ref.dtype)

    @pl.when(i < pl.num_programs(0) - 1)
    def _():
        _merge_prompt_tile(i, _frame_tile(head_ref, h_ref), a_ref, g1_ref, wugg_ref, poolw_ref, pscale_ref, wba_ref, wbp_ref, wout_ref,
                           g2_ref, wrt_ref, br_ref, h1_ref, xn2_ref, comb_ref, grow_ref, utail_ref, ext_ref, tail_ref)


def _merge_prompt_tile(i, h, a_ref, g1_ref, wugg_ref, poolw_ref, pscale_ref, wba_ref, wbp_ref, wout_ref,
                       g2_ref, wrt_ref, br_ref, h1_ref, xn2_ref, comb_ref, grow_ref, utail_ref, ext_ref, tail_ref):
    pw = LANES * len(POOL_WINDOWS)
    hist_rows = 16

    @pl.when(i == 0)
    def _():
        tail_ref[...] = jnp.zeros(tail_ref.shape, F32)

    xn = _rmsnorm(h, g1_ref[...]).astype(BF16)
    ugg = _dot(xn, wugg_ref[...])
    d = h.shape[1]
    u = ugg[:, 0:pw]
    ga = ugg[:, pw:pw + d]
    gb = ugg[:, pw + d:pw + 2 * d]
    utail_ref[...] = u[TM - hist_rows:TM, :]

    ext_ref[0:hist_rows, :] = tail_ref[...]
    ext_ref[hist_rows:hist_rows + TM, :] = u
    tail_ref[...] = u[TM - hist_rows:TM, :]

    pos = lax.broadcasted_iota(jnp.int32, (TM, LANES), 0) + i * TM - FRONT
    d_groups = []
    for g, w in enumerate(POOL_WINDOWS):
        lanes = slice(g * LANES, (g + 1) * LANES)
        tok = ext_ref[hist_rows:hist_rows + TM, lanes]
        acc = tok
        for back in range(1, w):
            acc = acc + ext_ref[hist_rows - back:hist_rows - back + TM, lanes]
        cnt = jnp.clip(pos + 1, 1, w).astype(F32)
        d_groups.append(acc / cnt - tok)

    _mix_and_route(h, xn, a_ref[...], d_groups, (ga, gb), poolw_ref, pscale_ref, wba_ref, wbp_ref, wout_ref,
                   g2_ref, wrt_ref, br_ref, h1_ref, xn2_ref, comb_ref, grow_ref)


def _merge_sample_kernel(h_ref, a_ref, hist_ref, g1_ref, wugg_ref, poolw_ref, pscale_ref, wba_ref, wbp_ref, wout_ref,
                         g2_ref, wrt_ref, br_ref, h1_in, xn2_in, comb_in, grow_in, h1_ref, xn2_ref, comb_ref, grow_ref, u_ref,
                         *, pos0, db):
    pw = LANES * len(POOL_WINDOWS)
    h = h_ref[...]
    xn = _rmsnorm(h, g1_ref[...]).astype(BF16)
    ugg = _dot(xn, wugg_ref[...])
    d = h.shape[1]
    u = ugg[:, 0:pw]
    ga = ugg[:, pw:pw + d]
    gb = ugg[:, pw + d:pw + 2 * d]
    u_ref[...] = u
    t_new = h.shape[0] // db
    n_hist = hist_ref.shape[0]

    def ext(e, lanes):
        if e < n_hist:
            return hist_ref[e][:, lanes]
        return u[(e - n_hist) * db:(e - n_hist + 1) * db, lanes]

    d_groups = []
    for g, w in enumerate(POOL_WINDOWS):
        lanes = slice(g * LANES, (g + 1) * LANES)
        per_t = []
        for t in range(t_new):
            tok = ext(n_hist + t, lanes)
            acc = tok
            for back in range(1, w):
                acc = acc + ext(n_hist + t - back, lanes)
            cnt = float(min(pos0 + t + 1, w))
            per_t.append(acc / cnt - tok)
        d_groups.append(jnp.concatenate(per_t, axis=0))

    _mix_and_route(h, xn, a_ref[...], d_groups, (ga, gb), poolw_ref, pscale_ref, wba_ref, wbp_ref, wout_ref,
                   g2_ref, wrt_ref, br_ref, h1_ref, xn2_ref, comb_ref, grow_ref)


def _merge_weights_specs(ws):
    return [_full(w.shape) for w in ws]


def _merge_prompt(head, x, a_bf, ws, n_all):
    seq, d = x.shape
    nt = seq // TM + 1
    assert n_all == (nt + 1) * TM
    aw = a_bf.shape[1]
    pw = LANES * len(POOL_WINDOWS)
    x_row = lambda i: (jnp.clip(i - 1, 0, nt - 2), 0)
    a_row = lambda i: (jnp.minimum(i, nt - 1), 0)
    dst = lambda i: jnp.where(i == 0, nt - 1, jnp.where(i == nt, nt, i - 1))
    row = lambda i: (dst(i), 0)
    return pl.pallas_call(
        _merge_prompt_kernel,
        grid=(nt + 1,),
        in_specs=[_full(head.shape), pl.BlockSpec((TM, d), x_row), pl.BlockSpec((TM, aw), a_row)]
        + _merge_weights_specs(ws),
        out_specs=[pl.BlockSpec((TM, d), row), pl.BlockSpec((TM, d), row), pl.BlockSpec((TM, LANES), row),
                   pl.BlockSpec((8, TM), lambda i: (0, dst(i))), pl.BlockSpec((16, pw), lambda i: (0, 0))],
        out_shape=[jax.ShapeDtypeStruct((n_all, d), F32), jax.ShapeDtypeStruct((n_all, d), BF16),
                   jax.ShapeDtypeStruct((n_all, LANES), F32), jax.ShapeDtypeStruct((8, n_all), F32),
                   jax.ShapeDtypeStruct((16, pw), F32)],
        scratch_shapes=[pltpu.VMEM((TM + 16, pw), F32), pltpu.VMEM((16, pw), F32)],
        compiler_params=pltpu.CompilerParams(dimension_semantics=("arbitrary",), vmem_limit_bytes=VMEM_LIMIT),
        name="merge_prompt",
    )(head, x, a_bf, *ws)


def _merge_sample(xs, a_bf, hist_t, ws, all_arrays, pos0, db):
    n, d = xs.shape
    assert n == TM
    aw = a_bf.shape[1]
    pw = LANES * len(POOL_WINDOWS)
    last = all_arrays[0].shape[0] // TM - 1
    row = lambda i: (i, 0)
    tail = lambda i: (last, 0)
    kern = functools.partial(_merge_sample_kernel, pos0=pos0, db=db)
    n_in = 3 + len(ws)
    return pl.pallas_call(
        kern,
        grid=(1,),
        in_specs=[pl.BlockSpec((TM, d), row), pl.BlockSpec((TM, aw), row), _full(hist_t.shape)]
        + _merge_weights_specs(ws) + [pl.BlockSpec(memory_space=pl.ANY)] * 4,
        out_specs=[pl.BlockSpec((TM, d), tail), pl.BlockSpec((TM, d), tail), pl.BlockSpec((TM, LANES), tail),
                   pl.BlockSpec((8, TM), lambda i: (0, last)), pl.BlockSpec((TM, pw), row)],
        out_shape=[jax.ShapeDtypeStruct(a.shape, a.dtype) for a in all_arrays] + [jax.ShapeDtypeStruct((n, pw), F32)],
        input_output_aliases={n_in + k: k for k in range(4)},
        compiler_params=pltpu.CompilerParams(dimension_semantics=("arbitrary",), vmem_limit_bytes=VMEM_LIMIT),
        name="merge_sample",
    )(xs, a_bf, hist_t, *ws, *all_arrays)


MOE_TM = 1024
MOE_WIN = 320
MOE_ALIGN = 16
MOE_SORTED = MOE_TM + N_EXPERT_GROUPS * MOE_ALIGN
MOE_ROWS = MOE_SORTED + MOE_WIN
MOE_WINDOWS = -(-MOE_TM // MOE_WIN)


def _moe_kernel(x_ref, comb_ref, grow_ref, h1_ref, wg_ref, wu_ref, wd_ref, gf_ref, o_ref, otail_ref,
                xs_ref, cs_ref, acc_ref, pt_ref, meta_ref, *, n_main):
    i = pl.program_id(0)
    e = pl.program_id(1)
    tm = x_ref.shape[0]
    lane = lax.broadcasted_iota(jnp.int32, (tm, LANES), 1)

    @pl.when(e == 0)
    def _():
        onehot_rows = grow_ref[...]
        comb = comb_ref[...]
        onehot_cols = jnp.where((lane >= N_EXPERTS) & (lane < N_EXPERTS + N_EXPERT_GROUPS), comb, 0.0)
        r_idx = lax.broadcasted_iota(jnp.int32, (tm, tm), 0)
        c_idx = lax.broadcasted_iota(jnp.int32, (tm, tm), 1)
        upper = jnp.where(r_idx < c_idx, 1.0, 0.0).astype(BF16)
        lower = jnp.where(c_idx < r_idx, 1.0, 0.0).astype(BF16)
        rank_rows = _dot(onehot_rows.astype(BF16), upper)
        rank_cols = _dot(lower, onehot_cols.astype(BF16))
        pos_row = jnp.zeros((1, tm), F32)
        pos_col = jnp.zeros((tm, 1), F32)
        start = jnp.int32(0)
        for g in range(N_EXPERT_GROUPS):
            count = jnp.sum(onehot_rows[g:g + 1, :]).astype(jnp.int32)
            meta_ref[g] = start
            meta_ref[N_EXPERT_GROUPS + g] = count
            startf = start.astype(F32)
            pos_row = pos_row + onehot_rows[g:g + 1, :] * (rank_rows[g:g + 1, :] + startf)
            lg = N_EXPERTS + g
            pos_col = pos_col + onehot_cols[:, lg:lg + 1] * (rank_cols[:, lg:lg + 1] + startf)
            start = start + ((count + (MOE_ALIGN - 1)) // MOE_ALIGN) * MOE_ALIGN
        s_rows = lax.broadcasted_iota(jnp.int32, (MOE_SORTED, tm), 0)
        perm = jnp.where(s_rows == pos_row.astype(jnp.int32), 1.0, 0.0).astype(BF16)
        s_cols = lax.broadcasted_iota(jnp.int32, (tm, MOE_SORTED), 1)
        pt_ref[...] = jnp.where(s_cols == pos_col.astype(jnp.int32), 1.0, 0.0).astype(BF16)
        xs_ref[0:MOE_SORTED, :] = _dot(perm, x_ref[...]).astype(BF16)
        xs_ref[MOE_SORTED:MOE_ROWS, :] = jnp.zeros((MOE_ROWS - MOE_SORTED, x_ref.shape[1]), BF16)
        cs_ref[0:MOE_SORTED, :] = _dot3_left(perm, _split3(comb))
        cs_ref[MOE_SORTED:MOE_ROWS, :] = jnp.zeros((MOE_ROWS - MOE_SORTED, LANES), F32)
        acc_ref[...] = jnp.zeros(acc_ref.shape, F32)

    g = e // EXPERTS_PER_GROUP
    start = meta_ref[g]
    count = meta_ref[N_EXPERT_GROUPS + g]
    wlane = lax.broadcasted_iota(jnp.int32, (MOE_WIN, LANES), 1)
    for w in range(MOE_WINDOWS):
        @pl.when(w * MOE_WIN < count)
        def _():
            rows = pl.ds(pl.multiple_of(start + w * MOE_WIN, MOE_ALIGN), MOE_WIN)
            x = xs_ref[rows, :]
            hdn = jax.nn.silu(_dot(x, wg_ref[...])) * _dot(x, wu_ref[...])
            out = _dot(hdn.astype(BF16), wd_ref[...])
            col = jnp.sum(jnp.where(wlane == e, cs_ref[rows, :], 0.0), axis=-1, keepdims=True)
            acc_ref[rows, :] += col * out

    @pl.when(e == pl.num_programs(1) - 1)
    def _():
        y = _dot(pt_ref[...], acc_ref[0:MOE_SORTED, :].astype(BF16))
        res = _rmsnorm(h1_ref[...] + y, gf_ref[...])

        @pl.when(i < n_main)
        def _():
            o_ref[...] = res

        @pl.when(i >= n_main)
        def _():
            otail_ref[...] = res


def _moe(xn2, comb, grow, h1, wg, wu, wd, gf, n_main_rows):
    n, d = xn2.shape
    nt = n // MOE_TM
    n_main = n_main_rows // MOE_TM
    assert n_main * MOE_TM == n_main_rows and nt == n_main + 1
    ne, _, de = wg.shape
    row = lambda i, e: (i, 0)
    return pl.pallas_call(
        functools.partial(_moe_kernel, n_main=n_main),
        grid=(nt, ne),
        in_specs=[pl.BlockSpec((MOE_TM, d), row), pl.BlockSpec((MOE_TM, LANES), row),
                  pl.BlockSpec((8, MOE_TM), lambda i, e: (0, i)), pl.BlockSpec((MOE_TM, d), row),
                  pl.BlockSpec((None, d, de), lambda i, e: (e, 0, 0)),
                  pl.BlockSpec((None, d, de), lambda i, e: (e, 0, 0)),
                  pl.BlockSpec((None, de, d), lambda i, e: (e, 0, 0)),
                  pl.BlockSpec(gf.shape, lambda i, e: (0, 0))],
        out_specs=[pl.BlockSpec((MOE_TM, d), lambda i, e: (jnp.minimum(i, n_main - 1), 0)),
                   pl.BlockSpec((MOE_TM, d), lambda i, e: (0, 0))],
        out_shape=[jax.ShapeDtypeStruct((n_main_rows, d), F32), jax.ShapeDtypeStruct((MOE_TM, d), F32)],
        scratch_shapes=[pltpu.VMEM((MOE_ROWS, d), BF16), pltpu.VMEM((MOE_ROWS, LANES), F32),
                        pltpu.VMEM((MOE_ROWS, d), F32), pltpu.VMEM((MOE_TM, MOE_SORTED), BF16),
                        pltpu.SMEM((2 * N_EXPERT_GROUPS,), jnp.int32)],
        compiler_params=pltpu.CompilerParams(dimension_semantics=("arbitrary", "arbitrary"),
                                             vmem_limit_bytes=VMEM_LIMIT),
        name="moe_grouped",
    )(xn2, comb, grow, h1, wg, wu, wd, gf)


def _placement_matrices():
    import numpy as np
    m = np.zeros((3, LANES, N_HEADS * LANES), np.float32)
    for h in range(N_HEADS):
        base = h * LANES + (HEAD_DIM if h % 2 == 0 else 0)
        for j in range(3):
            m[j, h, base + j] = 1.0
    return jnp.asarray(m, BF16)


def kernel(x_prompt, x_sample, cache_k, cache_v, cache_logf, state_pool, page_table, meta_tokens, norm1_g, w_in,
           b_forget, pool_w, pool_scale, w_branch_attn, w_branch_pool, w_out, norm2_g, w_router_group,
           b_router_group, w_router_expert, b_router_expert, w_gate, w_up, w_down, norm_f_g):
    import numpy as np
    depth = w_in.shape[0]
    assert depth == 1
    batch, seq, d = x_prompt.shape
    assert batch == 1
    db, t_new, _ = x_sample.shape
    assert db * t_new == TM
    aw = N_HEADS * HEAD_DIM
    pw = LANES * len(POOL_WINDOWS)
    seq_len = seq + N_META
    assert seq % MOE_TM == 0 and N_META <= 16
    tpad = seq + TM
    n_pool, page = cache_k.shape[1], cache_k.shape[2]
    n_pages = page_table.shape[1]
    past_len = n_pages * page
    assert page == LANES and t_new <= 8

    wl = w_in[0]
    w_qkvf = jnp.concatenate([wl[:, 0:3 * aw], jnp.pad(wl[:, 3 * aw:3 * aw + N_HEADS], ((0, 0), (0, LANES - N_HEADS)))],
                             axis=1).astype(BF16)
    w_ugg = wl[:, 3 * aw + N_HEADS:].astype(BF16)
    b_f = jnp.pad(b_forget[0], (0, LANES - N_HEADS)).reshape(1, LANES)
    g1 = norm1_g[0].reshape(1, d)
    g2 = norm2_g[0].reshape(1, d)
    gf = norm_f_g.reshape(1, d)
    n_r = N_EXPERT_GROUPS + N_EXPERTS
    w_rt = jnp.pad(jnp.concatenate([w_router_group[0], w_router_expert[0]], axis=1).T, ((0, 32 - n_r), (0, 0))).astype(BF16)
    b_r = jnp.pad(jnp.concatenate([b_router_group[0], b_router_expert[0]]), (0, 32 - n_r))
    b_r = jnp.broadcast_to(b_r[:, None], (32, TM))
    merge_ws = [g1, w_ugg, pool_w[0].astype(BF16), pool_scale[0].reshape(1, pw), w_branch_attn[0].astype(BF16),
                w_branch_pool[0].astype(BF16), w_out[0].astype(BF16), g2, w_rt, b_r]
    wg = w_gate[0].astype(BF16)
    wu = w_up[0].astype(BF16)
    wd = w_down[0].astype(BF16)

    tri_tm = jnp.asarray(np.tril(np.ones((TM, TM), np.float32)), BF16)
    place = _placement_matrices()

    head = jnp.concatenate([jnp.zeros((FRONT, d), x_prompt.dtype), meta_tokens.astype(x_prompt.dtype)], axis=0)
    k_p, v_p, lf_f, qt, kaug, vt, tot = _inproj_prompt(head, x_prompt[0], g1, w_qkvf, b_f, tri_tm, place)

    xs = jnp.transpose(x_sample, (1, 0, 2)).reshape(TM, d)
    q_s, k_s, v_s, lf_s, lft_s = _inproj_sample(xs, g1, w_qkvf, b_f)
    to_seq = lambda z: jnp.transpose(z.reshape(t_new, db, -1), (1, 0, 2))
    k_seq, v_seq = to_seq(k_s), to_seq(v_s)
    lft_pad = jnp.pad(jnp.transpose(lft_s.reshape(N_HEADS, t_new, db), (2, 0, 1)), ((0, 0), (0, 0), (0, LANES - t_new)))
    tri_new = jnp.asarray(np.triu(np.ones((LANES, LANES), np.float32)) * (np.arange(LANES) < t_new)[None, :], BF16)

    sfx_mat = jnp.asarray(np.concatenate([np.tril(np.ones((page, page), np.float32), -1),
                                          np.ones((page, page), np.float32)], axis=1), BF16)
    lf_pages = jnp.transpose(cache_logf[0], (0, 2, 1))

    kt_pages = jnp.transpose(cache_k[0], (0, 2, 3, 1)).reshape(n_pool, aw, page)
    vt_pages = jnp.transpose(cache_v[0], (0, 2, 3, 1)).reshape(n_pool, aw, page)
    a_p, a_s = _attention(qt, kaug, vt, tot[:, 0, 0:N_HEADS].reshape(-1), page_table, to_seq(q_s), k_seq, v_seq,
                          lft_pad, tri_new, sfx_mat, kt_pages, vt_pages, lf_pages)
    n_all = tpad + TM
    h1_a, xn2_a, comb_a, grow_a, utail = _merge_prompt(head, x_prompt[0], a_p, merge_ws, n_all)
    a_s_t = jnp.transpose(a_s, (1, 0, 2)).reshape(TM, aw)
    hist_t = jnp.transpose(state_pool[0], (1, 0, 2))
    h1_a, xn2_a, comb_a, grow_a, u_s = _merge_sample(xs, a_s_t, hist_t, merge_ws, (h1_a, xn2_a, comb_a, grow_a),
                                                     past_len, db)
    y_x, y_tail = _moe(xn2_a, comb_a, grow_a, h1_a, wg, wu, wd, gf, seq)

    y_prompt = y_x.reshape(1, seq, d)
    y_sample = to_seq(y_tail[TM:])
    k_prompt = k_p.reshape(1, 1, seq_len, N_HEADS, HEAD_DIM)
    v_prompt = v_p.reshape(1, 1, seq_len, N_HEADS, HEAD_DIM)
    logf_prompt = lf_f[FRONT:].reshape(1, 1, seq_len, N_HEADS)
    pool_prompt = utail[16 - POOL_HIST:16].reshape(1, 1, POOL_HIST, pw)
    k_sample = k_seq.reshape(1, db, t_new, N_HEADS, HEAD_DIM)
    v_sample = v_seq.reshape(1, db, t_new, N_HEADS, HEAD_DIM)
    logf_sample = to_seq(lf_s).reshape(1, db, t_new, N_HEADS)
    pool_sample = jnp.concatenate([state_pool[0].astype(F32), to_seq(u_s)], axis=1)[:, -POOL_HIST:].reshape(
        1, db, POOL_HIST, pw)
    return (y_prompt, y_sample, k_prompt, v_prompt, logf_prompt, pool_prompt,
            k_sample, v_sample, logf_sample, pool_sample)
```

```python
import functools

import jax
import jax.numpy as jnp
from jax import lax
from jax.experimental import pallas as pl
from jax.experimental.pallas import tpu as pltpu

F32 = jnp.float32
BF16 = jnp.bfloat16

N_HEADS = 8
HEAD_DIM = 64
N_META = 16
POOL_WINDOWS = (2, 4, 8, 16)
POOL_HIST = 15
N_EXPERT_GROUPS = 4
EXPERTS_PER_GROUP = 4
N_EXPERTS = 16
RMS_EPS = 1e-6

LANES = 128
TM = 512
BQ = 512
BK = 512
PAGES_PER_STEP = 16
SCORE_SLOTS = 3
VROWS = HEAD_DIM + 16
NEG_BIG = -1e30
FRONT = TM - N_META
LOG2E = 1.4426950408889634
VMEM_LIMIT = 52 * 1024 * 1024


def _rmsnorm(x, g):
    inv = lax.rsqrt(jnp.mean(x * x, axis=-1, keepdims=True) + RMS_EPS)
    return (x * inv) * g


def _log_sigmoid(x):
    return jnp.minimum(x, 0.0) - jnp.log1p(jnp.exp(-jnp.abs(x)))


def _split3(x):
    hi = x.astype(BF16)
    r1 = x - hi.astype(F32)
    mid = r1.astype(BF16)
    lo = (r1 - mid.astype(F32)).astype(BF16)
    return hi, mid, lo


def _dot(a, b):
    return jnp.dot(a, b, preferred_element_type=F32)


def _dot3(parts, m):
    return _dot(parts[0], m) + _dot(parts[1], m) + _dot(parts[2], m)


def _dot3_left(m, parts):
    return _dot(m, parts[0]) + _dot(m, parts[1]) + _dot(m, parts[2])


def _inproj_common(x, g_ref, w_ref, bf_ref):
    aw = N_HEADS * HEAD_DIM
    xn = _rmsnorm(x, g_ref[...]).astype(BF16)
    proj = _dot(xn, w_ref[...])
    q = proj[:, 0:aw]
    k = proj[:, aw:2 * aw]
    v = proj[:, 2 * aw:3 * aw]
    lf = _log_sigmoid(proj[:, 3 * aw:3 * aw + LANES] + bf_ref[...])
    return q, k, v, lf


def _frame_tile(head_ref, x_ref):
    return jnp.where(pl.program_id(0) == 0, head_ref[...], x_ref[...])


def _kv_row_copies(i, k_hbm, v_hbm, kbuf, vbuf, sems):
    first = [pltpu.make_async_copy(buf.at[pl.ds(FRONT, N_META)], hbm.at[pl.ds(0, N_META)], sems.at[n])
             for n, (buf, hbm) in enumerate(((kbuf, k_hbm), (vbuf, v_hbm)))]
    start = pl.multiple_of(N_META + (i - 1) * TM, N_META)
    later = [pltpu.make_async_copy(buf, hbm.at[pl.ds(start, TM)], sems.at[n])
             for n, (buf, hbm) in enumerate(((kbuf, k_hbm), (vbuf, v_hbm)))]
    return first, later


def _inproj_prompt_kernel(head_ref, x_ref, g_ref, w_ref, bf_ref, tri_ref, place_ref,
                          k_hbm, v_hbm, lf_ref, qt_ref, kaug_ref, vt_ref, tot_ref, kbuf, vbuf, sems):
    i = pl.program_id(0)
    q, k, v, lf = _inproj_common(_frame_tile(head_ref, x_ref), g_ref, w_ref, bf_ref)
    kbuf[...] = k
    vbuf[...] = v
    first, later = _kv_row_copies(i, k_hbm, v_hbm, kbuf, vbuf, sems)

    @pl.when(i == 0)
    def _():
        for cp in first:
            cp.start()

    @pl.when(i > 0)
    def _():
        for cp in later:
            cp.start()

    lf_ref[...] = lf[:, 0:N_HEADS]

    real = lax.broadcasted_iota(jnp.int32, (TM, LANES), 0) + pl.program_id(0) * TM >= FRONT
    cs = _dot3_left(tri_ref[...], _split3(jnp.where(real, lf, 0.0))) * LOG2E
    tot_ref[0] = cs[TM - 1:TM, :]

    c3 = _split3(jnp.where(real, cs, -NEG_BIG))
    aug = _dot(c3[0], place_ref[0]) + _dot(c3[1], place_ref[1]) + _dot(c3[2], place_ref[2])

    lane = lax.broadcasted_iota(jnp.int32, (TM, LANES), 1)
    low = lane < HEAD_DIM
    qt = (q * (HEAD_DIM ** -0.5 * LOG2E)).T
    vt = v.T
    row = lax.broadcasted_iota(jnp.int32, (HEAD_DIM, TM), 0)
    minus_ones = jnp.where(row < 3, -1.0, 0.0).astype(BF16)
    ones_row = jnp.where(lax.broadcasted_iota(jnp.int32, (VROWS - HEAD_DIM, TM), 0) == 0, 1.0, 0.0).astype(BF16)
    for h in range(N_HEADS):
        kp = k[:, (h // 2) * LANES:(h // 2 + 1) * LANES]
        own = low if h % 2 == 0 else jnp.logical_not(low)
        kaug_ref[h] = (jnp.where(own, kp, 0.0) + aug[:, h * LANES:(h + 1) * LANES]).astype(BF16)
        qh = qt[h * HEAD_DIM:(h + 1) * HEAD_DIM, :].astype(BF16)
        if h % 2 == 0:
            qt_ref[h, 0:HEAD_DIM, :] = qh
            qt_ref[h, HEAD_DIM:2 * HEAD_DIM, :] = minus_ones
        else:
            qt_ref[h, 0:HEAD_DIM, :] = minus_ones
            qt_ref[h, HEAD_DIM:2 * HEAD_DIM, :] = qh
        vt_ref[h, 0:HEAD_DIM, :] = vt[h * HEAD_DIM:(h + 1) * HEAD_DIM, :].astype(BF16)
        vt_ref[h, HEAD_DIM:VROWS, :] = ones_row

    @pl.when(i == 0)
    def _():
        for cp in first:
            cp.wait()

    @pl.when(i > 0)
    def _():
        for cp in later:
            cp.wait()


def _inproj_sample_kernel(x_ref, g_ref, w_ref, bf_ref, q_ref, k_ref, v_ref, lf_ref, lft_ref):
    q, k, v, lf = _inproj_common(x_ref[...], g_ref, w_ref, bf_ref)
    q_ref[...] = q
    k_ref[...] = k
    v_ref[...] = v
    lf_ref[...] = lf[:, 0:N_HEADS]
    lft_ref[...] = lf.T[0:N_HEADS, :]


def _full(shape):
    n = len(shape)
    return pl.BlockSpec(shape, lambda *_: (0,) * n)


def _inproj_prompt(head, x, g1, w_qkvf, b_f, tri, place):
    seq, d = x.shape
    tpad = seq + TM
    nt = tpad // TM
    aw = N_HEADS * HEAD_DIM
    row = lambda i: (i, 0)
    x_row = lambda i: (jnp.maximum(i - 1, 0), 0)
    return pl.pallas_call(
        _inproj_prompt_kernel,
        grid=(nt,),
        in_specs=[_full(head.shape), pl.BlockSpec((TM, d), x_row), _full(g1.shape), _full(w_qkvf.shape),
                  _full(b_f.shape), _full(tri.shape), _full(place.shape)],
        out_specs=[pl.BlockSpec(memory_space=pl.ANY), pl.BlockSpec(memory_space=pl.ANY),
                   pl.BlockSpec((TM, N_HEADS), row),
                   pl.BlockSpec((N_HEADS, LANES, TM), lambda i: (0, 0, i)),
                   pl.BlockSpec((N_HEADS, TM, LANES), lambda i: (0, i, 0)),
                   pl.BlockSpec((N_HEADS, VROWS, TM), lambda i: (0, 0, i)),
                   pl.BlockSpec((1, 1, LANES), lambda i: (i, 0, 0))],
        out_shape=[jax.ShapeDtypeStruct((seq + N_META, aw), F32), jax.ShapeDtypeStruct((seq + N_META, aw), F32),
                   jax.ShapeDtypeStruct((tpad, N_HEADS), F32),
                   jax.ShapeDtypeStruct((N_HEADS, LANES, tpad), BF16),
                   jax.ShapeDtypeStruct((N_HEADS, tpad, LANES), BF16),
                   jax.ShapeDtypeStruct((N_HEADS, VROWS, tpad), BF16),
                   jax.ShapeDtypeStruct((nt, 1, LANES), F32)],
        scratch_shapes=[pltpu.VMEM((TM, aw), F32), pltpu.VMEM((TM, aw), F32), pltpu.SemaphoreType.DMA((2,))],
        compiler_params=pltpu.CompilerParams(dimension_semantics=("arbitrary",), vmem_limit_bytes=VMEM_LIMIT),
        name="inproj_prompt",
    )(head, x, g1, w_qkvf, b_f, tri, place)


def _inproj_sample(xs, g1, w_qkvf, b_f):
    n, d = xs.shape
    nt = n // TM
    aw = N_HEADS * HEAD_DIM
    row = lambda i: (i, 0)
    return pl.pallas_call(
        _inproj_sample_kernel,
        grid=(nt,),
        in_specs=[pl.BlockSpec((TM, d), row), _full(g1.shape), _full(w_qkvf.shape), _full(b_f.shape)],
        out_specs=[pl.BlockSpec((TM, aw), row)] * 3 + [pl.BlockSpec((TM, N_HEADS), row),
                                                        pl.BlockSpec((N_HEADS, TM), lambda i: (0, i))],
        out_shape=[jax.ShapeDtypeStruct((n, aw), F32)] * 3 + [jax.ShapeDtypeStruct((n, N_HEADS), F32),
                                                               jax.ShapeDtypeStruct((N_HEADS, n), F32)],
        compiler_params=pltpu.CompilerParams(dimension_semantics=("arbitrary",), vmem_limit_bytes=VMEM_LIMIT),
        name="inproj_sample",
    )(xs, g1, w_qkvf, b_f)


def _flash_kernel(qi_ref, kj_ref, tot_ref, qt_ref, kaug_ref, vt_ref, o_ref, m_ref, acc_ref, z_ref, extra=()):
    p = pl.program_id(0)
    i = qi_ref[p]
    j = kj_ref[p]

    @pl.when(j == 0)
    def _():
        m_ref[...] = jnp.full(m_ref.shape, NEG_BIG, F32)
        acc_ref[...] = jnp.zeros(acc_ref.shape, F32)

    def scores(h):
        z_ref[h % SCORE_SLOTS] = _dot(kaug_ref[h], qt_ref[h])

    def softmax_pv(h, diagonal):
        z = z_ref[h % SCORE_SLOTS]
        if diagonal:
            s_pos = lax.broadcasted_iota(jnp.int32, (BK, BQ), 0)
            t_pos = lax.broadcasted_iota(jnp.int32, (BK, BQ), 1)
            z = jnp.where(s_pos <= t_pos, z, NEG_BIG)
        m = m_ref[h]
        m_new = jnp.maximum(m, jnp.max(z, axis=0, keepdims=True))
        pexp = jnp.exp2(z - m_new)
        alpha = jnp.exp2(m - m_new)
        acc_ref[h] = alpha * acc_ref[h] + _dot(vt_ref[h], pexp.astype(BF16))
        m_ref[h] = m_new + tot_ref[j * N_HEADS + h]

    def tile(diagonal):
        for h in range(SCORE_SLOTS - 1):
            scores(h)
        for h in range(N_HEADS):
            if h + SCORE_SLOTS - 1 < N_HEADS:
                scores(h + SCORE_SLOTS - 1)
            softmax_pv(h, diagonal)
            if h % 2 == 0 and h // 2 < len(extra):
                extra[h // 2]()

    @pl.when(j < i)
    def _():
        tile(False)

    @pl.when(j == i)
    def _():
        tile(True)
        outs = []
        for h in range(N_HEADS):
            a = acc_ref[h]
            outs.append(a[0:HEAD_DIM, :] / a[HEAD_DIM:HEAD_DIM + 1, :])
        o_ref[...] = jnp.concatenate(outs, axis=0).T.astype(BF16)


def _page_copies(step, slot, pt_ref, kt_hbm, vt_hbm, lf_hbm, kbuf, vbuf, sbuf, sems, n_pages, n_chunks):
    ppc = n_pages // n_chunks
    b = step // n_chunks
    c = step % n_chunks
    copies = []
    for jj in range(ppc):
        pg = pt_ref[b * n_pages + n_pages - 1 - (c * ppc + jj)]
        copies.append(pltpu.make_async_copy(kt_hbm.at[pg], kbuf.at[slot, jj], sems.at[0, slot]))
        copies.append(pltpu.make_async_copy(vt_hbm.at[pg], vbuf.at[slot, jj], sems.at[1, slot]))
        copies.append(pltpu.make_async_copy(lf_hbm.at[pg], sbuf.at[slot, jj], sems.at[2, slot]))
    return copies


def _head_mask():
    aw = N_HEADS * HEAD_DIM
    lane_head = lax.broadcasted_iota(jnp.int32, (N_HEADS, aw), 1) // HEAD_DIM
    return lane_head == lax.broadcasted_iota(jnp.int32, (N_HEADS, aw), 0)


def _decode_init(q_ref, kn_ref, vn_ref, lft_ref, tri_ref, qbd_ref, m_ref, l_ref, acc_ref, carry_ref):
    t_new = q_ref.shape[0]
    aw = N_HEADS * HEAD_DIM
    n_rows = t_new * N_HEADS
    head_mask = _head_mask()
    q = q_ref[...] * (HEAD_DIM ** -0.5)
    qbd = jnp.concatenate(
        [jnp.where(head_mask, jnp.broadcast_to(q[t:t + 1, :], (N_HEADS, aw)), 0.0) for t in range(t_new)], axis=0)
    qbd_ref[...] = qbd.astype(BF16)
    carry_ref[...] = jnp.zeros(carry_ref.shape, F32)

    cnew = _dot3(_split3(lft_ref[...]), tri_ref[...])
    cnew = jnp.concatenate([cnew] * t_new, axis=0)
    t_of_row = lax.broadcasted_iota(jnp.int32, (n_rows, 1), 0) // N_HEADS
    kn = kn_ref[...]
    vn = vn_ref[...]
    zs = []
    for t2 in range(t_new):
        zc = jnp.sum(qbd * kn[t2:t2 + 1, :], axis=-1, keepdims=True)
        zc = zc - cnew[:, t2:t2 + 1]
        zs.append(jnp.where(t_of_row >= t2, zc, NEG_BIG))
    m = functools.reduce(jnp.maximum, zs)
    l = jnp.zeros((n_rows, 1), F32)
    acc = jnp.zeros((n_rows, aw), F32)
    for t2 in range(t_new):
        pe = jnp.exp(zs[t2] - m)
        l = l + pe
        acc = acc + pe * vn[t2:t2 + 1, :]
    m_ref[...] = m
    l_ref[...] = l
    acc_ref[...] = acc


def _decode_chunk_stages(slot, qbd_ref, m_ref, l_ref, acc_ref, carry_ref, kbuf, vbuf, sbuf, sfxm_ref, zd_ref, pd_ref,
                         al_ref):
    ppc = kbuf.shape[1]
    page = kbuf.shape[3]
    t_new = qbd_ref.shape[0] // N_HEADS

    def scores():
        carry = carry_ref[...]
        logf = sbuf[slot].reshape(ppc * N_HEADS, page)
        sfx_all = _dot3(_split3(logf), sfxm_ref[...])
        biases = []
        for jj in range(ppc):
            sfx = sfx_all[jj * N_HEADS:(jj + 1) * N_HEADS, :]
            biases.append(sfx[:, 0:page] + carry)
            carry = carry + sfx[:, LANES:LANES + page]
        carry_ref[...] = carry
        bias = jnp.concatenate(biases, axis=1)
        ktc = jnp.concatenate([kbuf[slot, jj].astype(BF16) for jj in range(ppc)], axis=1)
        zd_ref[...] = _dot(qbd_ref[...], ktc) + jnp.concatenate([bias] * t_new, axis=0)

    def softmax():
        z = zd_ref[...]
        m_prev = m_ref[...]
        m_new = jnp.maximum(m_prev, jnp.max(z, axis=-1, keepdims=True))
        pexp = jnp.exp(z - m_new)
        alpha = jnp.exp(m_prev - m_new)
        l_ref[...] = alpha * l_ref[...] + jnp.sum(pexp, axis=-1, keepdims=True)
        pd_ref[...] = pexp.astype(BF16)
        al_ref[...] = alpha
        m_ref[...] = m_new

    def values():
        vtc = jnp.concatenate([vbuf[slot, jj].astype(BF16) for jj in range(ppc)], axis=1)
        upd = lax.dot_general(pd_ref[...], vtc, (((1,), (1,)), ((), ())), preferred_element_type=F32)
        acc_ref[...] = al_ref[...] * acc_ref[...] + upd

    return [scores, softmax, values]


def _decode_final(o_ref, l_ref, acc_ref):
    t_new = o_ref.shape[0]
    head_mask = _head_mask()
    a = acc_ref[...] / l_ref[...]
    outs = []
    for t in range(t_new):
        blk = jnp.where(head_mask, a[t * N_HEADS:(t + 1) * N_HEADS, :], 0.0)
        outs.append(jnp.sum(blk, axis=0, keepdims=True))
    o_ref[...] = jnp.concatenate(outs, axis=0).astype(o_ref.dtype)


def _attention_kernel(qi_ref, kj_ref, tot_ref, pt_ref,
                      qt_ref, kaug_ref, vt_ref, q_ref, kn_ref, vn_ref, lft_ref, tri_ref, sfxm_ref, kt_hbm, vt_hbm, lf_hbm,
                      o_ref, os_ref,
                      fm_ref, facc_ref, z_ref,
                      qbd_ref, dm_ref, dl_ref, dacc_ref, carry_ref, kbuf, vbuf, sbuf, zd_ref, pd_ref, al_ref, sems,
                      *, n_pages, n_chunks, n_dec, n_pairs):
    p = pl.program_id(0)

    def copies(step):
        return _page_copies(step, step % 2, pt_ref, kt_hbm, vt_hbm, lf_hbm, kbuf, vbuf, sbuf, sems,
                            n_pages, n_chunks)

    @pl.when(p == 0)
    def _():
        for cp in copies(p):
            cp.start()

    @pl.when(p + 1 < n_dec)
    def _():
        for cp in copies(p + 1):
            cp.start()

    decoding = p < n_dec
    c = p % n_chunks

    @pl.when(decoding)
    def _():
        for cp in copies(p):
            cp.wait()

    @pl.when(decoding & (c == 0))
    def _():
        _decode_init(q_ref, kn_ref, vn_ref, lft_ref, tri_ref, qbd_ref, dm_ref, dl_ref, dacc_ref, carry_ref)

    stages = _decode_chunk_stages(p % 2, qbd_ref, dm_ref, dl_ref, dacc_ref, carry_ref, kbuf, vbuf, sbuf, sfxm_ref,
                                  zd_ref, pd_ref, al_ref)

    @pl.when(p < n_pairs)
    def _():
        _flash_kernel(qi_ref, kj_ref, tot_ref, qt_ref, kaug_ref, vt_ref, o_ref, fm_ref, facc_ref, z_ref, extra=stages)

    @pl.when((p >= n_pairs) & decoding)
    def _():
        for stage in stages:
            stage()

    @pl.when(decoding & (c == n_chunks - 1))
    def _():
        _decode_final(os_ref, dl_ref, dacc_ref)


def _attention(qt, kaug, vt, tot_flat, page_table, q_s, k_new, v_new, lft_pad, tri_new, sfx_mat, kt_pages, vt_pages,
               lf_pages):
    tpad = qt.shape[2]
    nq = tpad // BQ
    pairs = [(i, j) for i in range(nq) for j in range(i + 1)]
    aw = N_HEADS * HEAD_DIM
    db, t_new, _ = q_s.shape
    n_pages = page_table.shape[1]
    page = kt_pages.shape[2]
    ppc = min(PAGES_PER_STEP, n_pages)
    n_chunks = n_pages // ppc
    n_dec = db * n_chunks
    n_pairs = len(pairs)
    n_steps = max(n_pairs, n_dec)
    pairs = pairs + [pairs[-1]] * (n_steps - n_pairs)
    qi = jnp.asarray([p[0] for p in pairs], jnp.int32)
    kj = jnp.asarray([p[1] for p in pairs], jnp.int32)
    n_rows = t_new * N_HEADS

    seq3 = lambda p, qi, kj, tot, pt: (jnp.minimum(p // n_chunks, db - 1), 0, 0)
    grid_spec = pltpu.PrefetchScalarGridSpec(
        num_scalar_prefetch=4,
        grid=(n_steps,),
        in_specs=[pl.BlockSpec((N_HEADS, LANES, BQ), lambda p, qi, kj, tot, pt: (0, 0, qi[p])),
                  pl.BlockSpec((N_HEADS, BK, LANES), lambda p, qi, kj, tot, pt: (0, kj[p], 0)),
                  pl.BlockSpec((N_HEADS, VROWS, BK), lambda p, qi, kj, tot, pt: (0, 0, kj[p])),
                  pl.BlockSpec((None, t_new, aw), seq3),
                  pl.BlockSpec((None, t_new, aw), seq3),
                  pl.BlockSpec((None, t_new, aw), seq3),
                  pl.BlockSpec((None, N_HEADS, LANES), seq3),
                  pl.BlockSpec(tri_new.shape, lambda p, qi, kj, tot, pt: (0, 0)),
                  pl.BlockSpec(sfx_mat.shape, lambda p, qi, kj, tot, pt: (0, 0)),
                  pl.BlockSpec(memory_space=pl.ANY),
                  pl.BlockSpec(memory_space=pl.ANY),
                  pl.BlockSpec(memory_space=pl.ANY)],
        out_specs=[pl.BlockSpec((BQ, aw), lambda p, qi, kj, tot, pt: (qi[p], 0)),
                   pl.BlockSpec((None, t_new, aw), seq3)],
        scratch_shapes=[pltpu.VMEM((N_HEADS, 1, BQ), F32), pltpu.VMEM((N_HEADS, VROWS, BQ), F32),
                        pltpu.VMEM((SCORE_SLOTS, BK, BQ), F32),
                        pltpu.VMEM((n_rows, aw), BF16), pltpu.VMEM((n_rows, 1), F32), pltpu.VMEM((n_rows, 1), F32),
                        pltpu.VMEM((n_rows, aw), F32), pltpu.VMEM((N_HEADS, LANES), F32),
                        pltpu.VMEM((2, ppc, aw, page), F32), pltpu.VMEM((2, ppc, aw, page), F32),
                        pltpu.VMEM((2, ppc, N_HEADS, page), F32),
                        pltpu.VMEM((n_rows, ppc * page), F32), pltpu.VMEM((n_rows, ppc * page), BF16),
                        pltpu.VMEM((n_rows, 1), F32), pltpu.SemaphoreType.DMA((3, 2))],
    )
    kern = functools.partial(_attention_kernel, n_pages=n_pages, n_chunks=n_chunks, n_dec=n_dec, n_pairs=n_pairs)
    return pl.pallas_call(
        kern,
        grid_spec=grid_spec,
        out_shape=[jax.ShapeDtypeStruct((tpad, aw), BF16), jax.ShapeDtypeStruct((db, t_new, aw), BF16)],
        compiler_params=pltpu.CompilerParams(dimension_semantics=("arbitrary",), vmem_limit_bytes=VMEM_LIMIT),
        name="attention",
    )(qi, kj, tot_flat, page_table.reshape(-1), qt, kaug, vt, q_s, k_new, v_new, lft_pad, tri_new, sfx_mat,
      kt_pages, vt_pages, lf_pages)


def _mix_and_route(h, xn, a_bf, d_groups, w_ugg_ga_gb, poolw_ref, pscale_ref, wba_ref, wbp_ref, wout_ref,
                   g2_ref, wrt_ref, br_ref, h1_ref, xn2_ref, comb_ref, grow_ref):
    ga, gb = w_ugg_ga_gb
    mixed = [_dot(d_groups[g].astype(BF16), poolw_ref[g]) for g in range(len(POOL_WINDOWS))]
    pooled = jnp.concatenate(mixed, axis=-1) * pscale_ref[...]
    m = jax.nn.sigmoid(ga) * _dot(a_bf, wba_ref[...]) + jax.nn.sigmoid(gb) * _dot(pooled.astype(BF16), wbp_ref[...])
    h1 = h + _dot(m.astype(BF16), wout_ref[...])
    h1_ref[...] = h1
    xn2 = _rmsnorm(h1, g2_ref[...])
    xn2_bf = xn2.astype(BF16)
    xn2_ref[...] = xn2_bf

    lt = lax.dot_general(wrt_ref[...], xn2_bf, (((1,), (1,)), ((), ())), preferred_element_type=F32) + br_ref[...]
    g_rows = [lt[g:g + 1, :] for g in range(N_EXPERT_GROUPS)]
    gmax = functools.reduce(jnp.maximum, g_rows)
    gsum = functools.reduce(lambda a, b: a + b, [jnp.exp(r - gmax) for r in g_rows])
    pg_sel = 1.0 / gsum
    gsel = jnp.full(gmax.shape, N_EXPERT_GROUPS - 1, jnp.int32)
    for g in range(N_EXPERT_GROUPS - 2, -1, -1):
        gsel = jnp.where(g_rows[g] == gmax, g, gsel)
    e_rows = []
    for k in range(EXPERTS_PER_GROUP):
        r = lt[N_EXPERT_GROUPS + k:N_EXPERT_GROUPS + k + 1, :]
        for g in range(1, N_EXPERT_GROUPS):
            base = N_EXPERT_GROUPS + g * EXPERTS_PER_GROUP + k
            r = jnp.where(gsel == g, lt[base:base + 1, :], r)
        e_rows.append(r)
    v1 = functools.reduce(jnp.maximum, e_rows)
    i1 = jnp.full(gsel.shape, EXPERTS_PER_GROUP - 1, jnp.int32)
    for k in range(EXPERTS_PER_GROUP - 2, -1, -1):
        i1 = jnp.where(e_rows[k] == v1, k, i1)
    rest = [jnp.where(i1 == k, -jnp.inf, e_rows[k]) for k in range(EXPERTS_PER_GROUP)]
    v2 = functools.reduce(jnp.maximum, rest)
    i2 = jnp.full(gsel.shape, EXPERTS_PER_GROUP - 1, jnp.int32)
    for k in range(EXPERTS_PER_GROUP - 2, -1, -1):
        i2 = jnp.where((rest[k] == v2) & (i1 != k), k, i2)
    e2 = jnp.exp(v2 - v1)
    w1 = (1.0 / (1.0 + e2)) * pg_sel
    w2 = (e2 / (1.0 + e2)) * pg_sel
    id1 = gsel * EXPERTS_PER_GROUP + i1
    id2 = gsel * EXPERTS_PER_GROUP + i2
    ntok = lt.shape[1]
    erow = lax.broadcasted_iota(jnp.int32, (LANES, ntok), 0)
    comb_t = jnp.where(erow == id1, w1, 0.0) + jnp.where(erow == id2, w2, 0.0)
    comb_t = comb_t + jnp.where(erow == N_EXPERTS + gsel, 1.0, 0.0)
    comb_ref[...] = comb_t.T
    grow_ref[...] = jnp.where(lax.broadcasted_iota(jnp.int32, (8, ntok), 0) == gsel, 1.0, 0.0)


def _merge_prompt_kernel(head_ref, h_ref, a_ref, g1_ref, wugg_ref, poolw_ref, pscale_ref, wba_ref, wbp_ref, wout_ref,
                         g2_ref, wrt_ref, br_ref, h1_ref, xn2_ref, comb_ref, grow_ref, utail_ref, ext_ref, tail_ref):
    i = pl.program_id(0)

    @pl.when(i == pl.num_programs(0) - 1)
    def _():
        h1_ref[...] = jnp.zeros(h1_ref.shape, h1_ref.dtype)
        xn2_ref[...] = jnp.zeros(xn2_ref.shape, xn2_ref.dtype)
        comb_ref[...] = jnp.zeros(comb_ref.shape, comb_ref.dtype)
        grow_ref[...] = jnp.zeros(grow_ref.shape, grow_ref.dtype)

    @pl.when(i < pl.num_programs(0) - 1)
    def _():
        _merge_prompt_tile(i, _frame_tile(head_ref, h_ref), a_ref, g1_ref, wugg_ref, poolw_ref, pscale_ref, wba_ref, wbp_ref, wout_ref,
                           g2_ref, wrt_ref, br_ref, h1_ref, xn2_ref, comb_ref, grow_ref, utail_ref, ext_ref, tail_ref)


def _merge_prompt_tile(i, h, a_ref, g1_ref, wugg_ref, poolw_ref, pscale_ref, wba_ref, wbp_ref, wout_ref,
                       g2_ref, wrt_ref, br_ref, h1_ref, xn2_ref, comb_ref, grow_ref, utail_ref, ext_ref, tail_ref):
    pw = LANES * len(POOL_WINDOWS)
    hist_rows = 16

    @pl.when(i == 0)
    def _():
        tail_ref[...] = jnp.zeros(tail_ref.shape, F32)

    xn = _rmsnorm(h, g1_ref[...]).astype(BF16)
    ugg = _dot(xn, wugg_ref[...])
    d = h.shape[1]
    u = ugg[:, 0:pw]
    ga = ugg[:, pw:pw + d]
    gb = ugg[:, pw + d:pw + 2 * d]
    utail_ref[...] = u[TM - hist_rows:TM, :]

    ext_ref[0:hist_rows, :] = tail_ref[...]
    ext_ref[hist_rows:hist_rows + TM, :] = u
    tail_ref[...] = u[TM - hist_rows:TM, :]

    pos = lax.broadcasted_iota(jnp.int32, (TM, LANES), 0) + i * TM - FRONT
    d_groups = []
    for g, w in enumerate(POOL_WINDOWS):
        lanes = slice(g * LANES, (g + 1) * LANES)
        tok = ext_ref[hist_rows:hist_rows + TM, lanes]
        acc = tok
        for back in range(1, w):
            acc = acc + ext_ref[hist_rows - back:hist_rows - back + TM, lanes]
        cnt = jnp.clip(pos + 1, 1, w).astype(F32)
        d_groups.append(acc / cnt - tok)

    _mix_and_route(h, xn, a_ref[...], d_groups, (ga, gb), poolw_ref, pscale_ref, wba_ref, wbp_ref, wout_ref,
                   g2_ref, wrt_ref, br_ref, h1_ref, xn2_ref, comb_ref, grow_ref)


def _merge_sample_kernel(h_ref, a_ref, hist_ref, g1_ref, wugg_ref, poolw_ref, pscale_ref, wba_ref, wbp_ref, wout_ref,
                         g2_ref, wrt_ref, br_ref, h1_in, xn2_in, comb_in, grow_in, h1_ref, xn2_ref, comb_ref, grow_ref, u_ref,
                         *, pos0, db):
    pw = LANES * len(POOL_WINDOWS)
    h = h_ref[...]
    xn = _rmsnorm(h, g1_ref[...]).astype(BF16)
    ugg = _dot(xn, wugg_ref[...])
    d = h.shape[1]
    u = ugg[:, 0:pw]
    ga = ugg[:, pw:pw + d]
    gb = ugg[:, pw + d:pw + 2 * d]
    u_ref[...] = u
    t_new = h.shape[0] // db
    n_hist = hist_ref.shape[0]

    def ext(e, lanes):
        if e < n_hist:
            return hist_ref[e][:, lanes]
        return u[(e - n_hist) * db:(e - n_hist + 1) * db, lanes]

    d_groups = []
    for g, w in enumerate(POOL_WINDOWS):
        lanes = slice(g * LANES, (g + 1) * LANES)
        per_t = []
        for t in range(t_new):
            tok = ext(n_hist + t, lanes)
            acc = tok
            for back in range(1, w):
                acc = acc + ext(n_hist + t - back, lanes)
            cnt = float(min(pos0 + t + 1, w))
            per_t.append(acc / cnt - tok)
        d_groups.append(jnp.concatenate(per_t, axis=0))

    _mix_and_route(h, xn, a_ref[...], d_groups, (ga, gb), poolw_ref, pscale_ref, wba_ref, wbp_ref, wout_ref,
                   g2_ref, wrt_ref, br_ref, h1_ref, xn2_ref, comb_ref, grow_ref)


def _merge_weights_specs(ws):
    return [_full(w.shape) for w in ws]


def _merge_prompt(head, x, a_bf, ws, n_all):
    seq, d = x.shape
    nt = seq // TM + 1
    assert n_all == (nt + 1) * TM
    aw = a_bf.shape[1]
    pw = LANES * len(POOL_WINDOWS)
    x_row = lambda i: (jnp.clip(i - 1, 0, nt - 2), 0)
    a_row = lambda i: (jnp.minimum(i, nt - 1), 0)
    dst = lambda i: jnp.where(i == 0, nt - 1, jnp.where(i == nt, nt, i - 1))
    row = lambda i: (dst(i), 0)
    return pl.pallas_call(
        _merge_prompt_kernel,
        grid=(nt + 1,),
        in_specs=[_full(head.shape), pl.BlockSpec((TM, d), x_row), pl.BlockSpec((TM, aw), a_row)]
        + _merge_weights_specs(ws),
        out_specs=[pl.BlockSpec((TM, d), row), pl.BlockSpec((TM, d), row), pl.BlockSpec((TM, LANES), row),
                   pl.BlockSpec((8, TM), lambda i: (0, dst(i))), pl.BlockSpec((16, pw), lambda i: (0, 0))],
        out_shape=[jax.ShapeDtypeStruct((n_all, d), F32), jax.ShapeDtypeStruct((n_all, d), BF16),
                   jax.ShapeDtypeStruct((n_all, LANES), F32), jax.ShapeDtypeStruct((8, n_all), F32),
                   jax.ShapeDtypeStruct((16, pw), F32)],
        scratch_shapes=[pltpu.VMEM((TM + 16, pw), F32), pltpu.VMEM((16, pw), F32)],
        compiler_params=pltpu.CompilerParams(dimension_semantics=("arbitrary",), vmem_limit_bytes=VMEM_LIMIT),
        name="merge_prompt",
    )(head, x, a_bf, *ws)


def _merge_sample(xs, a_bf, hist_t, ws, all_arrays, pos0, db):
    n, d = xs.shape
    assert n == TM
    aw = a_bf.shape[1]
    pw = LANES * len(POOL_WINDOWS)
    last = all_arrays[0].shape[0] // TM - 1
    row = lambda i: (i, 0)
    tail = lambda i: (last, 0)
    kern = functools.partial(_merge_sample_kernel, pos0=pos0, db=db)
    n_in = 3 + len(ws)
    return pl.pallas_call(
        kern,
        grid=(1,),
        in_specs=[pl.BlockSpec((TM, d), row), pl.BlockSpec((TM, aw), row), _full(hist_t.shape)]
        + _merge_weights_specs(ws) + [pl.BlockSpec(memory_space=pl.ANY)] * 4,
        out_specs=[pl.BlockSpec((TM, d), tail), pl.BlockSpec((TM, d), tail), pl.BlockSpec((TM, LANES), tail),
                   pl.BlockSpec((8, TM), lambda i: (0, last)), pl.BlockSpec((TM, pw), row)],
        out_shape=[jax.ShapeDtypeStruct(a.shape, a.dtype) for a in all_arrays] + [jax.ShapeDtypeStruct((n, pw), F32)],
        input_output_aliases={n_in + k: k for k in range(4)},
        compiler_params=pltpu.CompilerParams(dimension_semantics=("arbitrary",), vmem_limit_bytes=VMEM_LIMIT),
        name="merge_sample",
    )(xs, a_bf, hist_t, *ws, *all_arrays)


MOE_TM = 1024
MOE_WIN = 288
MOE_ALIGN = 16
MOE_SORTED = MOE_TM + N_EXPERT_GROUPS * MOE_ALIGN
MOE_ROWS = MOE_SORTED + MOE_WIN
MOE_WINDOWS = -(-MOE_TM // MOE_WIN)


def _moe_kernel(x_ref, comb_ref, grow_ref, h1_ref, wg_ref, wu_ref, wd_ref, gf_ref, o_ref, otail_ref,
                xs_ref, cs_ref, acc_ref, pt_ref, meta_ref, *, n_main):
    i = pl.program_id(0)
    e = pl.program_id(1)
    tm = x_ref.shape[0]
    lane = lax.broadcasted_iota(jnp.int32, (tm, LANES), 1)

    @pl.when(e == 0)
    def _():
        onehot_rows = grow_ref[...]
        comb = comb_ref[...]
        onehot_cols = jnp.where((lane >= N_EXPERTS) & (lane < N_EXPERTS + N_EXPERT_GROUPS), comb, 0.0)
        r_idx = lax.broadcasted_iota(jnp.int32, (tm, tm), 0)
        c_idx = lax.broadcasted_iota(jnp.int32, (tm, tm), 1)
        upper = jnp.where(r_idx < c_idx, 1.0, 0.0).astype(BF16)
        lower = jnp.where(c_idx < r_idx, 1.0, 0.0).astype(BF16)
        rank_rows = _dot(onehot_rows.astype(BF16), upper)
        rank_cols = _dot(lower, onehot_cols.astype(BF16))
        pos_row = jnp.zeros((1, tm), F32)
        pos_col = jnp.zeros((tm, 1), F32)
        start = jnp.int32(0)
        for g in range(N_EXPERT_GROUPS):
            count = jnp.sum(onehot_rows[g:g + 1, :]).astype(jnp.int32)
            meta_ref[g] = start
            meta_ref[N_EXPERT_GROUPS + g] = count
            startf = start.astype(F32)
            pos_row = pos_row + onehot_rows[g:g + 1, :] * (rank_rows[g:g + 1, :] + startf)
            lg = N_EXPERTS + g
            pos_col = pos_col + onehot_cols[:, lg:lg + 1] * (rank_cols[:, lg:lg + 1] + startf)
            start = start + ((count + (MOE_ALIGN - 1)) // MOE_ALIGN) * MOE_ALIGN
        s_rows = lax.broadcasted_iota(jnp.int32, (MOE_SORTED, tm), 0)
        perm = jnp.where(s_rows == pos_row.astype(jnp.int32), 1.0, 0.0).astype(BF16)
        s_cols = lax.broadcasted_iota(jnp.int32, (tm, MOE_SORTED), 1)
        pt_ref[...] = jnp.where(s_cols == pos_col.astype(jnp.int32), 1.0, 0.0).astype(BF16)
        xs_ref[0:MOE_SORTED, :] = _dot(perm, x_ref[...]).astype(BF16)
        xs_ref[MOE_SORTED:MOE_ROWS, :] = jnp.zeros((MOE_ROWS - MOE_SORTED, x_ref.shape[1]), BF16)
        cs_ref[0:MOE_SORTED, :] = _dot3_left(perm, _split3(comb))
        cs_ref[MOE_SORTED:MOE_ROWS, :] = jnp.zeros((MOE_ROWS - MOE_SORTED, LANES), F32)
        acc_ref[...] = jnp.zeros(acc_ref.shape, F32)

    g = e // EXPERTS_PER_GROUP
    start = meta_ref[g]
    count = meta_ref[N_EXPERT_GROUPS + g]
    wlane = lax.broadcasted_iota(jnp.int32, (MOE_WIN, LANES), 1)
    for w in range(MOE_WINDOWS):
        @pl.when(w * MOE_WIN < count)
        def _():
            rows = pl.ds(pl.multiple_of(start + w * MOE_WIN, MOE_ALIGN), MOE_WIN)
            x = xs_ref[rows, :]
            hdn = jax.nn.silu(_dot(x, wg_ref[...])) * _dot(x, wu_ref[...])
            out = _dot(hdn.astype(BF16), wd_ref[...])
            col = jnp.sum(jnp.where(wlane == e, cs_ref[rows, :], 0.0), axis=-1, keepdims=True)
            acc_ref[rows, :] += col * out

    @pl.when(e == pl.num_programs(1) - 1)
    def _():
        y = _dot(pt_ref[...], acc_ref[0:MOE_SORTED, :].astype(BF16))
        res = _rmsnorm(h1_ref[...] + y, gf_ref[...])

        @pl.when(i < n_main)
        def _():
            o_ref[...] = res

        @pl.when(i >= n_main)
        def _():
            otail_ref[...] = res


def _moe(xn2, comb, grow, h1, wg, wu, wd, gf, n_main_rows):
    n, d = xn2.shape
    nt = n // MOE_TM
    n_main = n_main_rows // MOE_TM
    assert n_main * MOE_TM == n_main_rows and nt == n_main + 1
    ne, _, de = wg.shape
    row = lambda i, e: (i, 0)
    return pl.pallas_call(
        functools.partial(_moe_kernel, n_main=n_main),
        grid=(nt, ne),
        in_specs=[pl.BlockSpec((MOE_TM, d), row), pl.BlockSpec((MOE_TM, LANES), row),
                  pl.BlockSpec((8, MOE_TM), lambda i, e: (0, i)), pl.BlockSpec((MOE_TM, d), row),
                  pl.BlockSpec((None, d, de), lambda i, e: (e, 0, 0)),
                  pl.BlockSpec((None, d, de), lambda i, e: (e, 0, 0)),
                  pl.BlockSpec((None, de, d), lambda i, e: (e, 0, 0)),
                  pl.BlockSpec(gf.shape, lambda i, e: (0, 0))],
        out_specs=[pl.BlockSpec((MOE_TM, d), lambda i, e: (jnp.minimum(i, n_main - 1), 0)),
                   pl.BlockSpec((MOE_TM, d), lambda i, e: (0, 0))],
        out_shape=[jax.ShapeDtypeStruct((n_main_rows, d), F32), jax.ShapeDtypeStruct((MOE_TM, d), F32)],
        scratch_shapes=[pltpu.VMEM((MOE_ROWS, d), BF16), pltpu.VMEM((MOE_ROWS, LANES), F32),
                        pltpu.VMEM((MOE_ROWS, d), F32), pltpu.VMEM((MOE_TM, MOE_SORTED), BF16),
                        pltpu.SMEM((2 * N_EXPERT_GROUPS,), jnp.int32)],
        compiler_params=pltpu.CompilerParams(dimension_semantics=("arbitrary", "arbitrary"),
                                             vmem_limit_bytes=VMEM_LIMIT),
        name="moe_grouped",
    )(xn2, comb, grow, h1, wg, wu, wd, gf)


def _placement_matrices():
    import numpy as np
    m = np.zeros((3, LANES, N_HEADS * LANES), np.float32)
    for h in range(N_HEADS):
        base = h * LANES + (HEAD_DIM if h % 2 == 0 else 0)
        for j in range(3):
            m[j, h, base + j] = 1.0
    return jnp.asarray(m, BF16)


def kernel(x_prompt, x_sample, cache_k, cache_v, cache_logf, state_pool, page_table, meta_tokens, norm1_g, w_in,
           b_forget, pool_w, pool_scale, w_branch_attn, w_branch_pool, w_out, norm2_g, w_router_group,
           b_router_group, w_router_expert, b_router_expert, w_gate, w_up, w_down, norm_f_g):
    import numpy as np
    depth = w_in.shape[0]
    assert depth == 1
    batch, seq, d = x_prompt.shape
    assert batch == 1
    db, t_new, _ = x_sample.shape
    assert db * t_new == TM
    aw = N_HEADS * HEAD_DIM
    pw = LANES * len(POOL_WINDOWS)
    seq_len = seq + N_META
    assert seq % MOE_TM == 0 and N_META <= 16
    tpad = seq + TM
    n_pool, page = cache_k.shape[1], cache_k.shape[2]
    n_pages = page_table.shape[1]
    past_len = n_pages * page
    assert page == LANES and t_new <= 8

    wl = w_in[0]
    w_qkvf = jnp.concatenate([wl[:, 0:3 * aw], jnp.pad(wl[:, 3 * aw:3 * aw + N_HEADS], ((0, 0), (0, LANES - N_HEADS)))],
                             axis=1).astype(BF16)
    w_ugg = wl[:, 3 * aw + N_HEADS:].astype(BF16)
    b_f = jnp.pad(b_forget[0], (0, LANES - N_HEADS)).reshape(1, LANES)
    g1 = norm1_g[0].reshape(1, d)
    g2 = norm2_g[0].reshape(1, d)
    gf = norm_f_g.reshape(1, d)
    n_r = N_EXPERT_GROUPS + N_EXPERTS
    w_rt = jnp.pad(jnp.concatenate([w_router_group[0], w_router_expert[0]], axis=1).T, ((0, 32 - n_r), (0, 0))).astype(BF16)
    b_r = jnp.pad(jnp.concatenate([b_router_group[0], b_router_expert[0]]), (0, 32 - n_r))
    b_r = jnp.broadcast_to(b_r[:, None], (32, TM))
    merge_ws = [g1, w_ugg, pool_w[0].astype(BF16), pool_scale[0].reshape(1, pw), w_branch_attn[0].astype(BF16),
                w_branch_pool[0].astype(BF16), w_out[0].astype(BF16), g2, w_rt, b_r]
    wg = w_gate[0].astype(BF16)
    wu = w_up[0].astype(BF16)
    wd = w_down[0].astype(BF16)

    tri_tm = jnp.asarray(np.tril(np.ones((TM, TM), np.float32)), BF16)
    place = _placement_matrices()

    head = jnp.concatenate([jnp.zeros((FRONT, d), x_prompt.dtype), meta_tokens.astype(x_prompt.dtype)], axis=0)
    k_p, v_p, lf_f, qt, kaug, vt, tot = _inproj_prompt(head, x_prompt[0], g1, w_qkvf, b_f, tri_tm, place)

    xs = jnp.transpose(x_sample, (1, 0, 2)).reshape(TM, d)
    q_s, k_s, v_s, lf_s, lft_s = _inproj_sample(xs, g1, w_qkvf, b_f)
    to_seq = lambda z: jnp.transpose(z.reshape(t_new, db, -1), (1, 0, 2))
    k_seq, v_seq = to_seq(k_s), to_seq(v_s)
    lft_pad = jnp.pad(jnp.transpose(lft_s.reshape(N_HEADS, t_new, db), (2, 0, 1)), ((0, 0), (0, 0), (0, LANES - t_new)))
    tri_new = jnp.asarray(np.triu(np.ones((LANES, LANES), np.float32)) * (np.arange(LANES) < t_new)[None, :], BF16)

    sfx_mat = jnp.asarray(np.concatenate([np.tril(np.ones((page, page), np.float32), -1),
                                          np.ones((page, page), np.float32)], axis=1), BF16)
    lf_pages = jnp.transpose(cache_logf[0], (0, 2, 1))

    kt_pages = jnp.transpose(cache_k[0], (0, 2, 3, 1)).reshape(n_pool, aw, page)
    vt_pages = jnp.transpose(cache_v[0], (0, 2, 3, 1)).reshape(n_pool, aw, page)
    a_p, a_s = _attention(qt, kaug, vt, tot[:, 0, 0:N_HEADS].reshape(-1), page_table, to_seq(q_s), k_seq, v_seq,
                          lft_pad, tri_new, sfx_mat, kt_pages, vt_pages, lf_pages)
    n_all = tpad + TM
    h1_a, xn2_a, comb_a, grow_a, utail = _merge_prompt(head, x_prompt[0], a_p, merge_ws, n_all)
    a_s_t = jnp.transpose(a_s, (1, 0, 2)).reshape(TM, aw)
    hist_t = jnp.transpose(state_pool[0], (1, 0, 2))
    h1_a, xn2_a, comb_a, grow_a, u_s = _merge_sample(xs, a_s_t, hist_t, merge_ws, (h1_a, xn2_a, comb_a, grow_a),
                                                     past_len, db)
    y_x, y_tail = _moe(xn2_a, comb_a, grow_a, h1_a, wg, wu, wd, gf, seq)

    y_prompt = y_x.reshape(1, seq, d)
    y_sample = to_seq(y_tail[TM:])
    k_prompt = k_p.reshape(1, 1, seq_len, N_HEADS, HEAD_DIM)
    v_prompt = v_p.reshape(1, 1, seq_len, N_HEADS, HEAD_DIM)
    logf_prompt = lf_f[FRONT:].reshape(1, 1, seq_len, N_HEADS)
    pool_prompt = utail[16 - POOL_HIST:16].reshape(1, 1, POOL_HIST, pw)
    k_sample = k_seq.reshape(1, db, t_new, N_HEADS, HEAD_DIM)
    v_sample = v_seq.reshape(1, db, t_new, N_HEADS, HEAD_DIM)
    logf_sample = to_seq(lf_s).reshape(1, db, t_new, N_HEADS)
    pool_sample = jnp.concatenate([state_pool[0].astype(F32), to_seq(u_s)], axis=1)[:, -POOL_HIST:].reshape(
        1, db, POOL_HIST, pw)
    return (y_prompt, y_sample, k_prompt, v_prompt, logf_prompt, pool_prompt,
            k_sample, v_sample, logf_sample, pool_sample)
```

```python
import functools

import jax
import jax.numpy as jnp
from jax import lax
from jax.experimental import pallas as pl
from jax.experimental.pallas import tpu as pltpu

F32 = jnp.float32
BF16 = jnp.bfloat16

N_HEADS = 8
HEAD_DIM = 64
N_META = 16
POOL_WINDOWS = (2, 4, 8, 16)
POOL_HIST = 15
N_EXPERT_GROUPS = 4
EXPERTS_PER_GROUP = 4
N_EXPERTS = 16
RMS_EPS = 1e-6

LANES = 128
TM = 512
BQ = 512
BK = 512
PAGES_PER_STEP = 16
PAGE_DMA_PRIORITY = 1
SCORE_SLOTS = 3
VROWS = HEAD_DIM + 16
NEG_BIG = -1e30
FRONT = TM - N_META
LOG2E = 1.4426950408889634
VMEM_LIMIT = 52 * 1024 * 1024


def _rmsnorm(x, g):
    inv = lax.rsqrt(jnp.mean(x * x, axis=-1, keepdims=True) + RMS_EPS)
    return (x * inv) * g


def _log_sigmoid(x):
    return jnp.minimum(x, 0.0) - jnp.log1p(jnp.exp(-jnp.abs(x)))


def _split3(x):
    hi = x.astype(BF16)
    r1 = x - hi.astype(F32)
    mid = r1.astype(BF16)
    lo = (r1 - mid.astype(F32)).astype(BF16)
    return hi, mid, lo


def _dot(a, b):
    return jnp.dot(a, b, preferred_element_type=F32)


def _dot3(parts, m):
    return _dot(parts[0], m) + _dot(parts[1], m) + _dot(parts[2], m)


def _dot3_left(m, parts):
    return _dot(m, parts[0]) + _dot(m, parts[1]) + _dot(m, parts[2])


def _inproj_common(x, g_ref, w_ref, bf_ref):
    aw = N_HEADS * HEAD_DIM
    xn = _rmsnorm(x, g_ref[...]).astype(BF16)
    proj = _dot(xn, w_ref[...])
    q = proj[:, 0:aw]
    k = proj[:, aw:2 * aw]
    v = proj[:, 2 * aw:3 * aw]
    lf = _log_sigmoid(proj[:, 3 * aw:3 * aw + LANES] + bf_ref[...])
    return q, k, v, lf


def _frame_tile(head_ref, x_ref):
    return jnp.where(pl.program_id(0) == 0, head_ref[...], x_ref[...])


def _kv_row_copies(i, k_hbm, v_hbm, kbuf, vbuf, sems):
    first = [pltpu.make_async_copy(buf.at[pl.ds(FRONT, N_META)], hbm.at[pl.ds(0, N_META)], sems.at[n])
             for n, (buf, hbm) in enumerate(((kbuf, k_hbm), (vbuf, v_hbm)))]
    start = pl.multiple_of(N_META + (i - 1) * TM, N_META)
    later = [pltpu.make_async_copy(buf, hbm.at[pl.ds(start, TM)], sems.at[n])
             for n, (buf, hbm) in enumerate(((kbuf, k_hbm), (vbuf, v_hbm)))]
    return first, later


def _inproj_prompt_kernel(head_ref, x_ref, g_ref, w_ref, bf_ref, tri_ref, place_ref,
                          k_hbm, v_hbm, lf_ref, qt_ref, kaug_ref, vt_ref, tot_ref, kbuf, vbuf, sems):
    i = pl.program_id(0)
    q, k, v, lf = _inproj_common(_frame_tile(head_ref, x_ref), g_ref, w_ref, bf_ref)
    kbuf[...] = k
    vbuf[...] = v
    first, later = _kv_row_copies(i, k_hbm, v_hbm, kbuf, vbuf, sems)

    @pl.when(i == 0)
    def _():
        for cp in first:
            cp.start()

    @pl.when(i > 0)
    def _():
        for cp in later:
            cp.start()

    lf_ref[...] = lf[:, 0:N_HEADS]

    real = lax.broadcasted_iota(jnp.int32, (TM, LANES), 0) + pl.program_id(0) * TM >= FRONT
    cs = _dot3_left(tri_ref[...], _split3(jnp.where(real, lf, 0.0))) * LOG2E
    tot_ref[0] = cs[TM - 1:TM, :]

    c3 = _split3(jnp.where(real, cs, -NEG_BIG))
    aug = _dot(c3[0], place_ref[0]) + _dot(c3[1], place_ref[1]) + _dot(c3[2], place_ref[2])

    lane = lax.broadcasted_iota(jnp.int32, (TM, LANES), 1)
    low = lane < HEAD_DIM
    qt = (q * (HEAD_DIM ** -0.5 * LOG2E)).T
    vt = v.T
    row = lax.broadcasted_iota(jnp.int32, (HEAD_DIM, TM), 0)
    minus_ones = jnp.where(row < 3, -1.0, 0.0).astype(BF16)
    ones_row = jnp.where(lax.broadcasted_iota(jnp.int32, (VROWS - HEAD_DIM, TM), 0) == 0, 1.0, 0.0).astype(BF16)
    for h in range(N_HEADS):
        kp = k[:, (h // 2) * LANES:(h // 2 + 1) * LANES]
        own = low if h % 2 == 0 else jnp.logical_not(low)
        kaug_ref[h] = (jnp.where(own, kp, 0.0) + aug[:, h * LANES:(h + 1) * LANES]).astype(BF16)
        qh = qt[h * HEAD_DIM:(h + 1) * HEAD_DIM, :].astype(BF16)
        if h % 2 == 0:
            qt_ref[h, 0:HEAD_DIM, :] = qh
            qt_ref[h, HEAD_DIM:2 * HEAD_DIM, :] = minus_ones
        else:
            qt_ref[h, 0:HEAD_DIM, :] = minus_ones
            qt_ref[h, HEAD_DIM:2 * HEAD_DIM, :] = qh
        vt_ref[h, 0:HEAD_DIM, :] = vt[h * HEAD_DIM:(h + 1) * HEAD_DIM, :].astype(BF16)
        vt_ref[h, HEAD_DIM:VROWS, :] = ones_row

    @pl.when(i == 0)
    def _():
        for cp in first:
            cp.wait()

    @pl.when(i > 0)
    def _():
        for cp in later:
            cp.wait()


def _inproj_sample_kernel(x_ref, g_ref, w_ref, bf_ref, q_ref, k_ref, v_ref, lf_ref, lft_ref):
    q, k, v, lf = _inproj_common(x_ref[...], g_ref, w_ref, bf_ref)
    q_ref[...] = q
    k_ref[...] = k
    v_ref[...] = v
    lf_ref[...] = lf[:, 0:N_HEADS]
    lft_ref[...] = lf.T[0:N_HEADS, :]


def _full(shape):
    n = len(shape)
    return pl.BlockSpec(shape, lambda *_: (0,) * n)


def _inproj_prompt(head, x, g1, w_qkvf, b_f, tri, place):
    seq, d = x.shape
    tpad = seq + TM
    nt = tpad // TM
    aw = N_HEADS * HEAD_DIM
    row = lambda i: (i, 0)
    x_row = lambda i: (jnp.maximum(i - 1, 0), 0)
    return pl.pallas_call(
        _inproj_prompt_kernel,
        grid=(nt,),
        in_specs=[_full(head.shape), pl.BlockSpec((TM, d), x_row), _full(g1.shape), _full(w_qkvf.shape),
                  _full(b_f.shape), _full(tri.shape), _full(place.shape)],
        out_specs=[pl.BlockSpec(memory_space=pl.ANY), pl.BlockSpec(memory_space=pl.ANY),
                   pl.BlockSpec((TM, N_HEADS), row),
                   pl.BlockSpec((N_HEADS, LANES, TM), lambda i: (0, 0, i)),
                   pl.BlockSpec((N_HEADS, TM, LANES), lambda i: (0, i, 0)),
                   pl.BlockSpec((N_HEADS, VROWS, TM), lambda i: (0, 0, i)),
                   pl.BlockSpec((1, 1, LANES), lambda i: (i, 0, 0))],
        out_shape=[jax.ShapeDtypeStruct((seq + N_META, aw), F32), jax.ShapeDtypeStruct((seq + N_META, aw), F32),
                   jax.ShapeDtypeStruct((tpad, N_HEADS), F32),
                   jax.ShapeDtypeStruct((N_HEADS, LANES, tpad), BF16),
                   jax.ShapeDtypeStruct((N_HEADS, tpad, LANES), BF16),
                   jax.ShapeDtypeStruct((N_HEADS, VROWS, tpad), BF16),
                   jax.ShapeDtypeStruct((nt, 1, LANES), F32)],
        scratch_shapes=[pltpu.VMEM((TM, aw), F32), pltpu.VMEM((TM, aw), F32), pltpu.SemaphoreType.DMA((2,))],
        compiler_params=pltpu.CompilerParams(dimension_semantics=("arbitrary",), vmem_limit_bytes=VMEM_LIMIT),
        name="inproj_prompt",
    )(head, x, g1, w_qkvf, b_f, tri, place)


def _inproj_sample(xs, g1, w_qkvf, b_f):
    n, d = xs.shape
    nt = n // TM
    aw = N_HEADS * HEAD_DIM
    row = lambda i: (i, 0)
    return pl.pallas_call(
        _inproj_sample_kernel,
        grid=(nt,),
        in_specs=[pl.BlockSpec((TM, d), row), _full(g1.shape), _full(w_qkvf.shape), _full(b_f.shape)],
        out_specs=[pl.BlockSpec((TM, aw), row)] * 3 + [pl.BlockSpec((TM, N_HEADS), row),
                                                        pl.BlockSpec((N_HEADS, TM), lambda i: (0, i))],
        out_shape=[jax.ShapeDtypeStruct((n, aw), F32)] * 3 + [jax.ShapeDtypeStruct((n, N_HEADS), F32),
                                                               jax.ShapeDtypeStruct((N_HEADS, n), F32)],
        compiler_params=pltpu.CompilerParams(dimension_semantics=("arbitrary",), vmem_limit_bytes=VMEM_LIMIT),
        name="inproj_sample",
    )(xs, g1, w_qkvf, b_f)


def _flash_kernel(qi_ref, kj_ref, tot_ref, qt_ref, kaug_ref, vt_ref, o_ref, m_ref, acc_ref, z_ref, extra=()):
    p = pl.program_id(0)
    i = qi_ref[p]
    j = kj_ref[p]

    @pl.when(j == 0)
    def _():
        m_ref[...] = jnp.full(m_ref.shape, NEG_BIG, F32)
        acc_ref[...] = jnp.zeros(acc_ref.shape, F32)

    def scores(h):
        z_ref[h % SCORE_SLOTS] = _dot(kaug_ref[h], qt_ref[h])

    def softmax_pv(h, diagonal):
        z = z_ref[h % SCORE_SLOTS]
        if diagonal:
            s_pos = lax.broadcasted_iota(jnp.int32, (BK, BQ), 0)
            t_pos = lax.broadcasted_iota(jnp.int32, (BK, BQ), 1)
            z = jnp.where(s_pos <= t_pos, z, NEG_BIG)
        m = m_ref[h]
        m_new = jnp.maximum(m, jnp.max(z, axis=0, keepdims=True))
        pexp = jnp.exp2(z - m_new)
        alpha = jnp.exp2(m - m_new)
        acc_ref[h] = alpha * acc_ref[h] + _dot(vt_ref[h], pexp.astype(BF16))
        m_ref[h] = m_new + tot_ref[j * N_HEADS + h]

    def tile(diagonal):
        for h in range(SCORE_SLOTS - 1):
            scores(h)
        for h in range(N_HEADS):
            if h + SCORE_SLOTS - 1 < N_HEADS:
                scores(h + SCORE_SLOTS - 1)
            softmax_pv(h, diagonal)
            if h % 2 == 0 and h // 2 < len(extra):
                extra[h // 2]()

    @pl.when(j < i)
    def _():
        tile(False)

    @pl.when(j == i)
    def _():
        tile(True)
        outs = []
        for h in range(N_HEADS):
            a = acc_ref[h]
            outs.append(a[0:HEAD_DIM, :] / a[HEAD_DIM:HEAD_DIM + 1, :])
        o_ref[...] = jnp.concatenate(outs, axis=0).T.astype(BF16)


def _page_copies(step, slot, pt_ref, kt_hbm, vt_hbm, lf_hbm, kbuf, vbuf, sbuf, sems, n_pages, n_chunks):
    ppc = n_pages // n_chunks
    b = step // n_chunks
    c = step % n_chunks
    copies = []
    for jj in range(ppc):
        pg = pt_ref[b * n_pages + n_pages - 1 - (c * ppc + jj)]
        copies.append(pltpu.make_async_copy(kt_hbm.at[pg], kbuf.at[slot, jj], sems.at[0, slot]))
        copies.append(pltpu.make_async_copy(vt_hbm.at[pg], vbuf.at[slot, jj], sems.at[1, slot]))
        copies.append(pltpu.make_async_copy(lf_hbm.at[pg], sbuf.at[slot, jj], sems.at[2, slot]))
    return copies


def _head_mask():
    aw = N_HEADS * HEAD_DIM
    lane_head = lax.broadcasted_iota(jnp.int32, (N_HEADS, aw), 1) // HEAD_DIM
    return lane_head == lax.broadcasted_iota(jnp.int32, (N_HEADS, aw), 0)


def _decode_init(q_ref, kn_ref, vn_ref, lft_ref, tri_ref, qbd_ref, m_ref, l_ref, acc_ref, carry_ref):
    t_new = q_ref.shape[0]
    aw = N_HEADS * HEAD_DIM
    n_rows = t_new * N_HEADS
    head_mask = _head_mask()
    q = q_ref[...] * (HEAD_DIM ** -0.5)
    qbd = jnp.concatenate(
        [jnp.where(head_mask, jnp.broadcast_to(q[t:t + 1, :], (N_HEADS, aw)), 0.0) for t in range(t_new)], axis=0)
    qbd_ref[...] = qbd.astype(BF16)
    carry_ref[...] = jnp.zeros(carry_ref.shape, F32)

    cnew = _dot3(_split3(lft_ref[...]), tri_ref[...])
    cnew = jnp.concatenate([cnew] * t_new, axis=0)
    t_of_row = lax.broadcasted_iota(jnp.int32, (n_rows, 1), 0) // N_HEADS
    kn = kn_ref[...]
    vn = vn_ref[...]
    zs = []
    for t2 in range(t_new):
        zc = jnp.sum(qbd * kn[t2:t2 + 1, :], axis=-1, keepdims=True)
        zc = zc - cnew[:, t2:t2 + 1]
        zs.append(jnp.where(t_of_row >= t2, zc, NEG_BIG))
    m = functools.reduce(jnp.maximum, zs)
    l = jnp.zeros((n_rows, 1), F32)
    acc = jnp.zeros((n_rows, aw), F32)
    for t2 in range(t_new):
        pe = jnp.exp(zs[t2] - m)
        l = l + pe
        acc = acc + pe * vn[t2:t2 + 1, :]
    m_ref[...] = m
    l_ref[...] = l
    acc_ref[...] = acc


def _decode_chunk_stages(slot, qbd_ref, m_ref, l_ref, acc_ref, carry_ref, kbuf, vbuf, sbuf, sfxm_ref, zd_ref, pd_ref,
                         al_ref):
    ppc = kbuf.shape[1]
    page = kbuf.shape[3]
    t_new = qbd_ref.shape[0] // N_HEADS

    def scores():
        carry = carry_ref[...]
        logf = sbuf[slot].reshape(ppc * N_HEADS, page)
        sfx_all = _dot3(_split3(logf), sfxm_ref[...])
        biases = []
        for jj in range(ppc):
            sfx = sfx_all[jj * N_HEADS:(jj + 1) * N_HEADS, :]
            biases.append(sfx[:, 0:page] + carry)
            carry = carry + sfx[:, LANES:LANES + page]
        carry_ref[...] = carry
        bias = jnp.concatenate(biases, axis=1)
        ktc = jnp.concatenate([kbuf[slot, jj].astype(BF16) for jj in range(ppc)], axis=1)
        zd_ref[...] = _dot(qbd_ref[...], ktc) + jnp.concatenate([bias] * t_new, axis=0)

    def softmax():
        z = zd_ref[...]
        m_prev = m_ref[...]
        m_new = jnp.maximum(m_prev, jnp.max(z, axis=-1, keepdims=True))
        pexp = jnp.exp(z - m_new)
        alpha = jnp.exp(m_prev - m_new)
        l_ref[...] = alpha * l_ref[...] + jnp.sum(pexp, axis=-1, keepdims=True)
        pd_ref[...] = pexp.astype(BF16)
        al_ref[...] = alpha
        m_ref[...] = m_new

    def values():
        vtc = jnp.concatenate([vbuf[slot, jj].astype(BF16) for jj in range(ppc)], axis=1)
        upd = lax.dot_general(pd_ref[...], vtc, (((1,), (1,)), ((), ())), preferred_element_type=F32)
        acc_ref[...] = al_ref[...] * acc_ref[...] + upd

    return [scores, softmax, values]


def _decode_final(o_ref, l_ref, acc_ref):
    t_new = o_ref.shape[0]
    head_mask = _head_mask()
    a = acc_ref[...] / l_ref[...]
    outs = []
    for t in range(t_new):
        blk = jnp.where(head_mask, a[t * N_HEADS:(t + 1) * N_HEADS, :], 0.0)
        outs.append(jnp.sum(blk, axis=0, keepdims=True))
    o_ref[...] = jnp.concatenate(outs, axis=0).astype(o_ref.dtype)


def _attention_kernel(qi_ref, kj_ref, tot_ref, pt_ref,
                      qt_ref, kaug_ref, vt_ref, q_ref, kn_ref, vn_ref, lft_ref, tri_ref, sfxm_ref, kt_hbm, vt_hbm, lf_hbm,
                      o_ref, os_ref,
                      fm_ref, facc_ref, z_ref,
                      qbd_ref, dm_ref, dl_ref, dacc_ref, carry_ref, kbuf, vbuf, sbuf, zd_ref, pd_ref, al_ref, sems,
                      *, n_pages, n_chunks, n_dec, n_pairs):
    p = pl.program_id(0)

    def copies(step):
        return _page_copies(step, step % 2, pt_ref, kt_hbm, vt_hbm, lf_hbm, kbuf, vbuf, sbuf, sems,
                            n_pages, n_chunks)

    @pl.when(p == 0)
    def _():
        for cp in copies(p):
            cp.start(priority=PAGE_DMA_PRIORITY)

    @pl.when(p + 1 < n_dec)
    def _():
        for cp in copies(p + 1):
            cp.start(priority=PAGE_DMA_PRIORITY)

    decoding = p < n_dec
    c = p % n_chunks

    @pl.when(decoding)
    def _():
        for cp in copies(p):
            cp.wait()

    @pl.when(decoding & (c == 0))
    def _():
        _decode_init(q_ref, kn_ref, vn_ref, lft_ref, tri_ref, qbd_ref, dm_ref, dl_ref, dacc_ref, carry_ref)

    stages = _decode_chunk_stages(p % 2, qbd_ref, dm_ref, dl_ref, dacc_ref, carry_ref, kbuf, vbuf, sbuf, sfxm_ref,
                                  zd_ref, pd_ref, al_ref)

    @pl.when(p < n_pairs)
    def _():
        _flash_kernel(qi_ref, kj_ref, tot_ref, qt_ref, kaug_ref, vt_ref, o_ref, fm_ref, facc_ref, z_ref, extra=stages)

    @pl.when((p >= n_pairs) & decoding)
    def _():
        for stage in stages:
            stage()

    @pl.when(decoding & (c == n_chunks - 1))
    def _():
        _decode_final(os_ref, dl_ref, dacc_ref)


def _attention(qt, kaug, vt, tot_flat, page_table, q_s, k_new, v_new, lft_pad, tri_new, sfx_mat, kt_pages, vt_pages,
               lf_pages):
    tpad = qt.shape[2]
    nq = tpad // BQ
    pairs = [(i, j) for i in range(nq) for j in range(i + 1)]
    aw = N_HEADS * HEAD_DIM
    db, t_new, _ = q_s.shape
    n_pages = page_table.shape[1]
    page = kt_pages.shape[2]
    ppc = min(PAGES_PER_STEP, n_pages)
    n_chunks = n_pages // ppc
    n_dec = db * n_chunks
    n_pairs = len(pairs)
    n_steps = max(n_pairs, n_dec)
    pairs = pairs + [pairs[-1]] * (n_steps - n_pairs)
    qi = jnp.asarray([p[0] for p in pairs], jnp.int32)
    kj = jnp.asarray([p[1] for p in pairs], jnp.int32)
    n_rows = t_new * N_HEADS

    seq3 = lambda p, qi, kj, tot, pt: (jnp.minimum(p // n_chunks, db - 1), 0, 0)
    grid_spec = pltpu.PrefetchScalarGridSpec(
        num_scalar_prefetch=4,
        grid=(n_steps,),
        in_specs=[pl.BlockSpec((N_HEADS, LANES, BQ), lambda p, qi, kj, tot, pt: (0, 0, qi[p])),
                  pl.BlockSpec((N_HEADS, BK, LANES), lambda p, qi, kj, tot, pt: (0, kj[p], 0)),
                  pl.BlockSpec((N_HEADS, VROWS, BK), lambda p, qi, kj, tot, pt: (0, 0, kj[p])),
                  pl.BlockSpec((None, t_new, aw), seq3),
                  pl.BlockSpec((None, t_new, aw), seq3),
                  pl.BlockSpec((None, t_new, aw), seq3),
                  pl.BlockSpec((None, N_HEADS, LANES), seq3),
                  pl.BlockSpec(tri_new.shape, lambda p, qi, kj, tot, pt: (0, 0)),
                  pl.BlockSpec(sfx_mat.shape, lambda p, qi, kj, tot, pt: (0, 0)),
                  pl.BlockSpec(memory_space=pl.ANY),
                  pl.BlockSpec(memory_space=pl.ANY),
                  pl.BlockSpec(memory_space=pl.ANY)],
        out_specs=[pl.BlockSpec((BQ, aw), lambda p, qi, kj, tot, pt: (qi[p], 0)),
                   pl.BlockSpec((None, t_new, aw), seq3)],
        scratch_shapes=[pltpu.VMEM((N_HEADS, 1, BQ), F32), pltpu.VMEM((N_HEADS, VROWS, BQ), F32),
                        pltpu.VMEM((SCORE_SLOTS, BK, BQ), F32),
                        pltpu.VMEM((n_rows, aw), BF16), pltpu.VMEM((n_rows, 1), F32), pltpu.VMEM((n_rows, 1), F32),
                        pltpu.VMEM((n_rows, aw), F32), pltpu.VMEM((N_HEADS, LANES), F32),
                        pltpu.VMEM((2, ppc, aw, page), F32), pltpu.VMEM((2, ppc, aw, page), F32),
                        pltpu.VMEM((2, ppc, N_HEADS, page), F32),
                        pltpu.VMEM((n_rows, ppc * page), F32), pltpu.VMEM((n_rows, ppc * page), BF16),
                        pltpu.VMEM((n_rows, 1), F32), pltpu.SemaphoreType.DMA((3, 2))],
    )
    kern = functools.partial(_attention_kernel, n_pages=n_pages, n_chunks=n_chunks, n_dec=n_dec, n_pairs=n_pairs)
    return pl.pallas_call(
        kern,
        grid_spec=grid_spec,
        out_shape=[jax.ShapeDtypeStruct((tpad, aw), BF16), jax.ShapeDtypeStruct((db, t_new, aw), BF16)],
        compiler_params=pltpu.CompilerParams(dimension_semantics=("arbitrary",), vmem_limit_bytes=VMEM_LIMIT),
        name="attention",
    )(qi, kj, tot_flat, page_table.reshape(-1), qt, kaug, vt, q_s, k_new, v_new, lft_pad, tri_new, sfx_mat,
      kt_pages, vt_pages, lf_pages)


def _mix_and_route(h, xn, a_bf, d_groups, w_ugg_ga_gb, poolw_ref, pscale_ref, wba_ref, wbp_ref, wout_ref,
                   g2_ref, wrt_ref, br_ref, h1_ref, xn2_ref, comb_ref, grow_ref):
    ga, gb = w_ugg_ga_gb
    mixed = [_dot(d_groups[g].astype(BF16), poolw_ref[g]) for g in range(len(POOL_WINDOWS))]
    pooled = jnp.concatenate(mixed, axis=-1) * pscale_ref[...]
    m = jax.nn.sigmoid(ga) * _dot(a_bf, wba_ref[...]) + jax.nn.sigmoid(gb) * _dot(pooled.astype(BF16), wbp_ref[...])
    h1 = h + _dot(m.astype(BF16), wout_ref[...])
    h1_ref[...] = h1
    xn2 = _rmsnorm(h1, g2_ref[...])
    xn2_bf = xn2.astype(BF16)
    xn2_ref[...] = xn2_bf

    lt = lax.dot_general(wrt_ref[...], xn2_bf, (((1,), (1,)), ((), ())), preferred_element_type=F32) + br_ref[...]
    g_rows = [lt[g:g + 1, :] for g in range(N_EXPERT_GROUPS)]
    gmax = functools.reduce(jnp.maximum, g_rows)
    gsum = functools.reduce(lambda a, b: a + b, [jnp.exp(r - gmax) for r in g_rows])
    pg_sel = 1.0 / gsum
    gsel = jnp.full(gmax.shape, N_EXPERT_GROUPS - 1, jnp.int32)
    for g in range(N_EXPERT_GROUPS - 2, -1, -1):
        gsel = jnp.where(g_rows[g] == gmax, g, gsel)
    e_rows = []
    for k in range(EXPERTS_PER_GROUP):
        r = lt[N_EXPERT_GROUPS + k:N_EXPERT_GROUPS + k + 1, :]
        for g in range(1, N_EXPERT_GROUPS):
            base = N_EXPERT_GROUPS + g * EXPERTS_PER_GROUP + k
            r = jnp.where(gsel == g, lt[base:base + 1, :], r)
        e_rows.append(r)
    v1 = functools.reduce(jnp.maximum, e_rows)
    i1 = jnp.full(gsel.shape, EXPERTS_PER_GROUP - 1, jnp.int32)
    for k in range(EXPERTS_PER_GROUP - 2, -1, -1):
        i1 = jnp.where(e_rows[k] == v1, k, i1)
    rest = [jnp.where(i1 == k, -jnp.inf, e_rows[k]) for k in range(EXPERTS_PER_GROUP)]
    v2 = functools.reduce(jnp.maximum, rest)
    i2 = jnp.full(gsel.shape, EXPERTS_PER_GROUP - 1, jnp.int32)
    for k in range(EXPERTS_PER_GROUP - 2, -1, -1):
        i2 = jnp.where((rest[k] == v2) & (i1 != k), k, i2)
    e2 = jnp.exp(v2 - v1)
    w1 = (1.0 / (1.0 + e2)) * pg_sel
    w2 = (e2 / (1.0 + e2)) * pg_sel
    id1 = gsel * EXPERTS_PER_GROUP + i1
    id2 = gsel * EXPERTS_PER_GROUP + i2
    ntok = lt.shape[1]
    erow = lax.broadcasted_iota(jnp.int32, (LANES, ntok), 0)
    comb_t = jnp.where(erow == id1, w1, 0.0) + jnp.where(erow == id2, w2, 0.0)
    comb_t = comb_t + jnp.where(erow == N_EXPERTS + gsel, 1.0, 0.0)
    comb_ref[...] = comb_t.T
    grow_ref[...] = jnp.where(lax.broadcasted_iota(jnp.int32, (8, ntok), 0) == gsel, 1.0, 0.0)


def _merge_prompt_kernel(head_ref, h_ref, a_ref, g1_ref, wugg_ref, poolw_ref, pscale_ref, wba_ref, wbp_ref, wout_ref,
                         g2_ref, wrt_ref, br_ref, h1_ref, xn2_ref, comb_ref, grow_ref, utail_ref, ext_ref, tail_ref):
    i = pl.program_id(0)

    @pl.when(i == pl.num_programs(0) - 1)
    def _():
        h1_ref[...] = jnp.zeros(h1_ref.shape, h1_ref.dtype)
        xn2_ref[...] = jnp.zeros(xn2_ref.shape, xn2_ref.dtype)
        comb_ref[...] = jnp.zeros(comb_ref.shape, comb_ref.dtype)
        grow_ref[...] = jnp.zeros(grow_ref.shape, grow_ref.dtype)

    @pl.when(i < pl.num_programs(0) - 1)
    def _():
        _merge_prompt_tile(i, _frame_tile(head_ref, h_ref), a_ref, g1_ref, wugg_ref, poolw_ref, pscale_ref, wba_ref, wbp_ref, wout_ref,
                           g2_ref, wrt_ref, br_ref, h1_ref, xn2_ref, comb_ref, grow_ref, utail_ref, ext_ref, tail_ref)


def _merge_prompt_tile(i, h, a_ref, g1_ref, wugg_ref, poolw_ref, pscale_ref, wba_ref, wbp_ref, wout_ref,
                       g2_ref, wrt_ref, br_ref, h1_ref, xn2_ref, comb_ref, grow_ref, utail_ref, ext_ref, tail_ref):
    pw = LANES * len(POOL_WINDOWS)
    hist_rows = 16

    @pl.when(i == 0)
    def _():
        tail_ref[...] = jnp.zeros(tail_ref.shape, F32)

    xn = _rmsnorm(h, g1_ref[...]).astype(BF16)
    ugg = _dot(xn, wugg_ref[...])
    d = h.shape[1]
    u = ugg[:, 0:pw]
    ga = ugg[:, pw:pw + d]
    gb = ugg[:, pw + d:pw + 2 * d]
    utail_ref[...] = u[TM - hist_rows:TM, :]

    ext_ref[0:hist_rows, :] = tail_ref[...]
    ext_ref[hist_rows:hist_rows + TM, :] = u
    tail_ref[...] = u[TM - hist_rows:TM, :]

    pos = lax.broadcasted_iota(jnp.int32, (TM, LANES), 0) + i * TM - FRONT
    d_groups = []
    for g, w in enumerate(POOL_WINDOWS):
        lanes = slice(g * LANES, (g + 1) * LANES)
        tok = ext_ref[hist_rows:hist_rows + TM, lanes]
        acc = tok
        for back in range(1, w):
            acc = acc + ext_ref[hist_rows - back:hist_rows - back + TM, lanes]
        cnt = jnp.clip(pos + 1, 1, w).astype(F32)
        d_groups.append(acc / cnt - tok)

    _mix_and_route(h, xn, a_ref[...], d_groups, (ga, gb), poolw_ref, pscale_ref, wba_ref, wbp_ref, wout_ref,
                   g2_ref, wrt_ref, br_ref, h1_ref, xn2_ref, comb_ref, grow_ref)


def _merge_sample_kernel(h_ref, a_ref, hist_ref, g1_ref, wugg_ref, poolw_ref, pscale_ref, wba_ref, wbp_ref, wout_ref,
                         g2_ref, wrt_ref, br_ref, h1_in, xn2_in, comb_in, grow_in, h1_ref, xn2_ref, comb_ref, grow_ref, u_ref,
                         *, pos0, db):
    pw = LANES * len(POOL_WINDOWS)
    h = h_ref[...]
    xn = _rmsnorm(h, g1_ref[...]).astype(BF16)
    ugg = _dot(xn, wugg_ref[...])
    d = h.shape[1]
    u = ugg[:, 0:pw]
    ga = ugg[:, pw:pw + d]
    gb = ugg[:, pw + d:pw + 2 * d]
    u_ref[...] = u
    t_new = h.shape[0] // db
    n_hist = hist_ref.shape[0]

    def ext(e, lanes):
        if e < n_hist:
            return hist_ref[e][:, lanes]
        return u[(e - n_hist) * db:(e - n_hist + 1) * db, lanes]

    d_groups = []
    for g, w in enumerate(POOL_WINDOWS):
        lanes = slice(g * LANES, (g + 1) * LANES)
        per_t = []
        for t in range(t_new):
            tok = ext(n_hist + t, lanes)
            acc = tok
            for back in range(1, w):
                acc = acc + ext(n_hist + t - back, lanes)
            cnt = float(min(pos0 + t + 1, w))
            per_t.append(acc / cnt - tok)
        d_groups.append(jnp.concatenate(per_t, axis=0))

    _mix_and_route(h, xn, a_ref[...], d_groups, (ga, gb), poolw_ref, pscale_ref, wba_ref, wbp_ref, wout_ref,
                   g2_ref, wrt_ref, br_ref, h1_ref, xn2_ref, comb_ref, grow_ref)


def _merge_weights_specs(ws):
    return [_full(w.shape) for w in ws]


def _merge_prompt(head, x, a_bf, ws, n_all):
    seq, d = x.shape
    nt = seq // TM + 1
    assert n_all == (nt + 1) * TM
    aw = a_bf.shape[1]
    pw = LANES * len(POOL_WINDOWS)
    x_row = lambda i: (jnp.clip(i - 1, 0, nt - 2), 0)
    a_row = lambda i: (jnp.minimum(i, nt - 1), 0)
    dst = lambda i: jnp.where(i == 0, nt - 1, jnp.where(i == nt, nt, i - 1))
    row = lambda i: (dst(i), 0)
    return pl.pallas_call(
        _merge_prompt_kernel,
        grid=(nt + 1,),
        in_specs=[_full(head.shape), pl.BlockSpec((TM, d), x_row), pl.BlockSpec((TM, aw), a_row)]
        + _merge_weights_specs(ws),
        out_specs=[pl.BlockSpec((TM, d), row), pl.BlockSpec((TM, d), row), pl.BlockSpec((TM, LANES), row),
                   pl.BlockSpec((8, TM), lambda i: (0, dst(i))), pl.BlockSpec((16, pw), lambda i: (0, 0))],
        out_shape=[jax.ShapeDtypeStruct((n_all, d), F32), jax.ShapeDtypeStruct((n_all, d), BF16),
                   jax.ShapeDtypeStruct((n_all, LANES), F32), jax.ShapeDtypeStruct((8, n_all), F32),
                   jax.ShapeDtypeStruct((16, pw), F32)],
        scratch_shapes=[pltpu.VMEM((TM + 16, pw), F32), pltpu.VMEM((16, pw), F32)],
        compiler_params=pltpu.CompilerParams(dimension_semantics=("arbitrary",), vmem_limit_bytes=VMEM_LIMIT),
        name="merge_prompt",
    )(head, x, a_bf, *ws)


def _merge_sample(xs, a_bf, hist_t, ws, all_arrays, pos0, db):
    n, d = xs.shape
    assert n == TM
    aw = a_bf.shape[1]
    pw = LANES * len(POOL_WINDOWS)
    last = all_arrays[0].shape[0] // TM - 1
    row = lambda i: (i, 0)
    tail = lambda i: (last, 0)
    kern = functools.partial(_merge_sample_kernel, pos0=pos0, db=db)
    n_in = 3 + len(ws)
    return pl.pallas_call(
        kern,
        grid=(1,),
        in_specs=[pl.BlockSpec((TM, d), row), pl.BlockSpec((TM, aw), row), _full(hist_t.shape)]
        + _merge_weights_specs(ws) + [pl.BlockSpec(memory_space=pl.ANY)] * 4,
        out_specs=[pl.BlockSpec((TM, d), tail), pl.BlockSpec((TM, d), tail), pl.BlockSpec((TM, LANES), tail),
                   pl.BlockSpec((8, TM), lambda i: (0, last)), pl.BlockSpec((TM, pw), row)],
        out_shape=[jax.ShapeDtypeStruct(a.shape, a.dtype) for a in all_arrays] + [jax.ShapeDtypeStruct((n, pw), F32)],
        input_output_aliases={n_in + k: k for k in range(4)},
        compiler_params=pltpu.CompilerParams(dimension_semantics=("arbitrary",), vmem_limit_bytes=VMEM_LIMIT),
        name="merge_sample",
    )(xs, a_bf, hist_t, *ws, *all_arrays)


MOE_TM = 1024
MOE_WIN = 288
MOE_ALIGN = 16
MOE_SORTED = MOE_TM + N_EXPERT_GROUPS * MOE_ALIGN
MOE_ROWS = MOE_SORTED + MOE_WIN
MOE_WINDOWS = -(-MOE_TM // MOE_WIN)


def _moe_kernel(x_ref, comb_ref, grow_ref, h1_ref, wg_ref, wu_ref, wd_ref, gf_ref, o_ref, otail_ref,
                xs_ref, cs_ref, acc_ref, pt_ref, meta_ref, *, n_main):
    i = pl.program_id(0)
    e = pl.program_id(1)
    tm = x_ref.shape[0]
    lane = lax.broadcasted_iota(jnp.int32, (tm, LANES), 1)

    @pl.when(e == 0)
    def _():
        onehot_rows = grow_ref[...]
        comb = comb_ref[...]
        onehot_cols = jnp.where((lane >= N_EXPERTS) & (lane < N_EXPERTS + N_EXPERT_GROUPS), comb, 0.0)
        r_idx = lax.broadcasted_iota(jnp.int32, (tm, tm), 0)
        c_idx = lax.broadcasted_iota(jnp.int32, (tm, tm), 1)
        upper = jnp.where(r_idx < c_idx, 1.0, 0.0).astype(BF16)
        lower = jnp.where(c_idx < r_idx, 1.0, 0.0).astype(BF16)
        rank_rows = _dot(onehot_rows.astype(BF16), upper)
        rank_cols = _dot(lower, onehot_cols.astype(BF16))
        pos_row = jnp.zeros((1, tm), F32)
        pos_col = jnp.zeros((tm, 1), F32)
        start = jnp.int32(0)
        for g in range(N_EXPERT_GROUPS):
            count = jnp.sum(onehot_rows[g:g + 1, :]).astype(jnp.int32)
            meta_ref[g] = start
            meta_ref[N_EXPERT_GROUPS + g] = count
            startf = start.astype(F32)
            pos_row = pos_row + onehot_rows[g:g + 1, :] * (rank_rows[g:g + 1, :] + startf)
            lg = N_EXPERTS + g
            pos_col = pos_col + onehot_cols[:, lg:lg + 1] * (rank_cols[:, lg:lg + 1] + startf)
            start = start + ((count + (MOE_ALIGN - 1)) // MOE_ALIGN) * MOE_ALIGN
        s_rows = lax.broadcasted_iota(jnp.int32, (MOE_SORTED, tm), 0)
        perm = jnp.where(s_rows == pos_row.astype(jnp.int32), 1.0, 0.0).astype(BF16)
        s_cols = lax.broadcasted_iota(jnp.int32, (tm, MOE_SORTED), 1)
        pt_ref[...] = jnp.where(s_cols == pos_col.astype(jnp.int32), 1.0, 0.0).astype(BF16)
        xs_ref[0:MOE_SORTED, :] = _dot(perm, x_ref[...]).astype(BF16)
        xs_ref[MOE_SORTED:MOE_ROWS, :] = jnp.zeros((MOE_ROWS - MOE_SORTED, x_ref.shape[1]), BF16)
        cs_ref[0:MOE_SORTED, :] = _dot3_left(perm, _split3(comb))
        cs_ref[MOE_SORTED:MOE_ROWS, :] = jnp.zeros((MOE_ROWS - MOE_SORTED, LANES), F32)
        acc_ref[...] = jnp.zeros(acc_ref.shape, F32)

    g = e // EXPERTS_PER_GROUP
    start = meta_ref[g]
    count = meta_ref[N_EXPERT_GROUPS + g]
    wlane = lax.broadcasted_iota(jnp.int32, (MOE_WIN, LANES), 1)
    for w in range(MOE_WINDOWS):
        @pl.when(w * MOE_WIN < count)
        def _():
            rows = pl.ds(pl.multiple_of(start + w * MOE_WIN, MOE_ALIGN), MOE_WIN)
            x = xs_ref[rows, :]
            hdn = jax.nn.silu(_dot(x, wg_ref[...])) * _dot(x, wu_ref[...])
            out = _dot(hdn.astype(BF16), wd_ref[...])
            col = jnp.sum(jnp.where(wlane == e, cs_ref[rows, :], 0.0), axis=-1, keepdims=True)
            acc_ref[rows, :] += col * out

    @pl.when(e == pl.num_programs(1) - 1)
    def _():
        y = _dot(pt_ref[...], acc_ref[0:MOE_SORTED, :].astype(BF16))
        res = _rmsnorm(h1_ref[...] + y, gf_ref[...])

        @pl.when(i < n_main)
        def _():
            o_ref[...] = res

        @pl.when(i >= n_main)
        def _():
            otail_ref[...] = res


def _moe(xn2, comb, grow, h1, wg, wu, wd, gf, n_main_rows):
    n, d = xn2.shape
    nt = n // MOE_TM
    n_main = n_main_rows // MOE_TM
    assert n_main * MOE_TM == n_main_rows and nt == n_main + 1
    ne, _, de = wg.shape
    row = lambda i, e: (i, 0)
    return pl.pallas_call(
        functools.partial(_moe_kernel, n_main=n_main),
        grid=(nt, ne),
        in_specs=[pl.BlockSpec((MOE_TM, d), row), pl.BlockSpec((MOE_TM, LANES), row),
                  pl.BlockSpec((8, MOE_TM), lambda i, e: (0, i)), pl.BlockSpec((MOE_TM, d), row),
                  pl.BlockSpec((None, d, de), lambda i, e: (e, 0, 0)),
                  pl.BlockSpec((None, d, de), lambda i, e: (e, 0, 0)),
                  pl.BlockSpec((None, de, d), lambda i, e: (e, 0, 0)),
                  pl.BlockSpec(gf.shape, lambda i, e: (0, 0))],
        out_specs=[pl.BlockSpec((MOE_TM, d), lambda i, e: (jnp.minimum(i, n_main - 1), 0)),
                   pl.BlockSpec((MOE_TM, d), lambda i, e: (0, 0))],
        out_shape=[jax.ShapeDtypeStruct((n_main_rows, d), F32), jax.ShapeDtypeStruct((MOE_TM, d), F32)],
        scratch_shapes=[pltpu.VMEM((MOE_ROWS, d), BF16), pltpu.VMEM((MOE_ROWS, LANES), F32),
                        pltpu.VMEM((MOE_ROWS, d), F32), pltpu.VMEM((MOE_TM, MOE_SORTED), BF16),
                        pltpu.SMEM((2 * N_EXPERT_GROUPS,), jnp.int32)],
        compiler_params=pltpu.CompilerParams(dimension_semantics=("arbitrary", "arbitrary"),
                                             vmem_limit_bytes=VMEM_LIMIT),
        name="moe_grouped",
    )(xn2, comb, grow, h1, wg, wu, wd, gf)


def _placement_matrices():
    import numpy as np
    m = np.zeros((3, LANES, N_HEADS * LANES), np.float32)
    for h in range(N_HEADS):
        base = h * LANES + (HEAD_DIM if h % 2 == 0 else 0)
        for j in range(3):
            m[j, h, base + j] = 1.0
    return jnp.asarray(m, BF16)


def kernel(x_prompt, x_sample, cache_k, cache_v, cache_logf, state_pool, page_table, meta_tokens, norm1_g, w_in,
           b_forget, pool_w, pool_scale, w_branch_attn, w_branch_pool, w_out, norm2_g, w_router_group,
           b_router_group, w_router_expert, b_router_expert, w_gate, w_up, w_down, norm_f_g):
    import numpy as np
    depth = w_in.shape[0]
    assert depth == 1
    batch, seq, d = x_prompt.shape
    assert batch == 1
    db, t_new, _ = x_sample.shape
    assert db * t_new == TM
    aw = N_HEADS * HEAD_DIM
    pw = LANES * len(POOL_WINDOWS)
    seq_len = seq + N_META
    assert seq % MOE_TM == 0 and N_META <= 16
    tpad = seq + TM
    n_pool, page = cache_k.shape[1], cache_k.shape[2]
    n_pages = page_table.shape[1]
    past_len = n_pages * page
    assert page == LANES and t_new <= 8

    wl = w_in[0]
    w_qkvf = jnp.concatenate([wl[:, 0:3 * aw], jnp.pad(wl[:, 3 * aw:3 * aw + N_HEADS], ((0, 0), (0, LANES - N_HEADS)))],
                             axis=1).astype(BF16)
    w_ugg = wl[:, 3 * aw + N_HEADS:].astype(BF16)
    b_f = jnp.pad(b_forget[0], (0, LANES - N_HEADS)).reshape(1, LANES)
    g1 = norm1_g[0].reshape(1, d)
    g2 = norm2_g[0].reshape(1, d)
    gf = norm_f_g.reshape(1, d)
    n_r = N_EXPERT_GROUPS + N_EXPERTS
    w_rt = jnp.pad(jnp.concatenate([w_router_group[0], w_router_expert[0]], axis=1).T, ((0, 32 - n_r), (0, 0))).astype(BF16)
    b_r = jnp.pad(jnp.concatenate([b_router_group[0], b_router_expert[0]]), (0, 32 - n_r))
    b_r = jnp.broadcast_to(b_r[:, None], (32, TM))
    merge_ws = [g1, w_ugg, pool_w[0].astype(BF16), pool_scale[0].reshape(1, pw), w_branch_attn[0].astype(BF16),
                w_branch_pool[0].astype(BF16), w_out[0].astype(BF16), g2, w_rt, b_r]
    wg = w_gate[0].astype(BF16)
    wu = w_up[0].astype(BF16)
    wd = w_down[0].astype(BF16)

    tri_tm = jnp.asarray(np.tril(np.ones((TM, TM), np.float32)), BF16)
    place = _placement_matrices()

    head = jnp.concatenate([jnp.zeros((FRONT, d), x_prompt.dtype), meta_tokens.astype(x_prompt.dtype)], axis=0)
    k_p, v_p, lf_f, qt, kaug, vt, tot = _inproj_prompt(head, x_prompt[0], g1, w_qkvf, b_f, tri_tm, place)

    xs = jnp.transpose(x_sample, (1, 0, 2)).reshape(TM, d)
    q_s, k_s, v_s, lf_s, lft_s = _inproj_sample(xs, g1, w_qkvf, b_f)
    to_seq = lambda z: jnp.transpose(z.reshape(t_new, db, -1), (1, 0, 2))
    k_seq, v_seq = to_seq(k_s), to_seq(v_s)
    lft_pad = jnp.pad(jnp.transpose(lft_s.reshape(N_HEADS, t_new, db), (2, 0, 1)), ((0, 0), (0, 0), (0, LANES - t_new)))
    tri_new = jnp.asarray(np.triu(np.ones((LANES, LANES), np.float32)) * (np.arange(LANES) < t_new)[None, :], BF16)

    sfx_mat = jnp.asarray(np.concatenate([np.tril(np.ones((page, page), np.float32), -1),
                                          np.ones((page, page), np.float32)], axis=1), BF16)
    lf_pages = jnp.transpose(cache_logf[0], (0, 2, 1))

    kt_pages = jnp.transpose(cache_k[0], (0, 2, 3, 1)).reshape(n_pool, aw, page)
    vt_pages = jnp.transpose(cache_v[0], (0, 2, 3, 1)).reshape(n_pool, aw, page)
    a_p, a_s = _attention(qt, kaug, vt, tot[:, 0, 0:N_HEADS].reshape(-1), page_table, to_seq(q_s), k_seq, v_seq,
                          lft_pad, tri_new, sfx_mat, kt_pages, vt_pages, lf_pages)
    n_all = tpad + TM
    h1_a, xn2_a, comb_a, grow_a, utail = _merge_prompt(head, x_prompt[0], a_p, merge_ws, n_all)
    a_s_t = jnp.transpose(a_s, (1, 0, 2)).reshape(TM, aw)
    hist_t = jnp.transpose(state_pool[0], (1, 0, 2))
    h1_a, xn2_a, comb_a, grow_a, u_s = _merge_sample(xs, a_s_t, hist_t, merge_ws, (h1_a, xn2_a, comb_a, grow_a),
                                                     past_len, db)
    y_x, y_tail = _moe(xn2_a, comb_a, grow_a, h1_a, wg, wu, wd, gf, seq)

    y_prompt = y_x.reshape(1, seq, d)
    y_sample = to_seq(y_tail[TM:])
    k_prompt = k_p.reshape(1, 1, seq_len, N_HEADS, HEAD_DIM)
    v_prompt = v_p.reshape(1, 1, seq_len, N_HEADS, HEAD_DIM)
    logf_prompt = lf_f[FRONT:].reshape(1, 1, seq_len, N_HEADS)
    pool_prompt = utail[16 - POOL_HIST:16].reshape(1, 1, POOL_HIST, pw)
    k_sample = k_seq.reshape(1, db, t_new, N_HEADS, HEAD_DIM)
    v_sample = v_seq.reshape(1, db, t_new, N_HEADS, HEAD_DIM)
    logf_sample = to_seq(lf_s).reshape(1, db, t_new, N_HEADS)
    pool_sample = jnp.concatenate([state_pool[0].astype(F32), to_seq(u_s)], axis=1)[:, -POOL_HIST:].reshape(
        1, db, POOL_HIST, pw)
    return (y_prompt, y_sample, k_prompt, v_prompt, logf_prompt, pool_prompt,
            k_sample, v_sample, logf_sample, pool_sample)
```
